```python
import math
import jax, jax.numpy as jnp
from jax import lax
import numpy as np

D_MODEL = 1024
BATCH = 8
SEQ = 4096
DEPTH = 1

NORM_EPS = 1e-6
D_FF = 2816
GDN_HEADS = 4
GDN_HEAD_DIM = 128
GDN_WIDTH = GDN_HEADS * GDN_HEAD_DIM
GDN_CHUNK = 64
CONV_WIDTH = 4
MOBA_HEADS = 8
MOBA_HEAD_DIM = 64
MOBA_WIDTH = MOBA_HEADS * MOBA_HEAD_DIM
MOBA_BLOCK = 256
MOBA_TOPK = 3
MOBA_Q_CHUNK = 128
ROPE_THETA = 500000.0
ROPE_DIM = MOBA_HEAD_DIM // 4
MIX_WIDTH = GDN_WIDTH + MOBA_WIDTH
OFF_GDN_QKV = 3 * GDN_WIDTH
OFF_GDN_Z = 4 * GDN_WIDTH
OFF_GDN_A = OFF_GDN_Z + GDN_HEADS
OFF_GDN_B = OFF_GDN_A + GDN_HEADS
OFF_MOBA_Q = OFF_GDN_B + MOBA_WIDTH
OFF_MOBA_K = OFF_MOBA_Q + MOBA_WIDTH
IN_PROJ_DIM = OFF_MOBA_K + MOBA_WIDTH

kernel_name = "hymba_gdn_moba_macaron"


def rms_norm(x, g, eps=NORM_EPS):
    xf = x.astype(jnp.float32)
    y = xf * lax.rsqrt(jnp.mean(xf * xf, axis=-1, keepdims=True) + eps)
    return (y * g).astype(x.dtype)


def l2_normalize(x, eps=NORM_EPS):
    return x * lax.rsqrt(jnp.sum(x * x, axis=-1, keepdims=True) + eps)


def swiglu_ffn(x, norm_g, w_gate, w_up, w_down):
    h = rms_norm(x, norm_g)
    return (jax.nn.silu(h @ w_gate) * (h @ w_up)) @ w_down


def causal_dwconv(x, w):
    kw = w.shape[0]
    return lax.conv_general_dilated(
        x, w[:, None, :].astype(x.dtype), window_strides=(1,), padding=[(kw - 1, 0)],
        dimension_numbers=('NWC', 'WIO', 'NWC'), feature_group_count=x.shape[-1])


def partial_rope(x, pos):
    half = ROPE_DIM // 2
    inv_freq = jnp.power(jnp.float32(ROPE_THETA), -jnp.arange(half, dtype=jnp.float32) * 2.0 / ROPE_DIM)
    ang = pos.astype(jnp.float32)[:, None] * inv_freq[None, :]
    cos, sin = jnp.cos(ang)[:, None, :], jnp.sin(ang)[:, None, :]
    xr = x[..., :ROPE_DIM].astype(jnp.float32)
    x1, x2 = xr[..., :half], xr[..., half:]
    rot = jnp.concatenate([x1 * cos - x2 * sin, x2 * cos + x1 * sin], axis=-1)
    return jnp.concatenate([rot.astype(x.dtype), x[..., ROPE_DIM:]], axis=-1)


def gated_deltanet(q, k, v, z, a, b, a_log, dt_bias, out_gain):
    bsz, seq, _ = q.shape
    H, Dh, C = GDN_HEADS, GDN_HEAD_DIM, GDN_CHUNK
    n_chunks = seq // C
    f32 = jnp.float32
    q = l2_normalize(q.reshape(bsz, seq, H, Dh).astype(f32)) * (Dh ** -0.5)
    k = l2_normalize(k.reshape(bsz, seq, H, Dh).astype(f32))
    v = v.reshape(bsz, seq, H, Dh).astype(f32)
    beta = jax.nn.sigmoid(b.astype(f32))
    g = -jnp.exp(a_log.astype(f32)) * jax.nn.softplus(a.astype(f32) + dt_bias.astype(f32))

    def chunks(t):
        return t.reshape(bsz, n_chunks, C, H, -1).transpose(0, 3, 1, 2, 4)

    qc, kc, vc = chunks(q), chunks(k), chunks(v)
    beta_c = chunks(beta[..., None])[..., 0]
    gcum = jnp.cumsum(chunks(g[..., None])[..., 0], axis=-1)
    idx = jnp.arange(C)
    causal = idx[:, None] >= idx[None, :]
    strict = idx[:, None] > idx[None, :]
    decay = jnp.exp(jnp.where(causal, gcum[..., :, None] - gcum[..., None, :], -jnp.inf))
    k_beta = kc * beta_c[..., None]
    lower = jnp.where(strict, jnp.einsum('bhnid,bhnjd->bhnij', k_beta, kc) * decay, 0.0)
    rhs = jnp.concatenate([vc * beta_c[..., None], k_beta * jnp.exp(gcum)[..., None]], axis=-1)
    sol = lax.linalg.triangular_solve(jnp.eye(C, dtype=f32) + lower, rhs,
                                      left_side=True, lower=True, unit_diagonal=True)
    u, w = sol[..., :Dh], sol[..., Dh:]
    intra = jnp.einsum('bhnid,bhnjd->bhnij', qc, kc) * decay
    q_dec = qc * jnp.exp(gcum)[..., None]
    k_dec = kc * jnp.exp(gcum[..., -1:] - gcum)[..., None]
    g_last = jnp.exp(gcum[..., -1])
    xs = tuple(jnp.moveaxis(t, 2, 0) for t in (q_dec, k_dec, u, w, intra, g_last))

    def step(state, inp):
        q_i, k_i, u_i, w_i, a_i, gl_i = inp
        v_new = u_i - jnp.einsum('bhck,bhkv->bhcv', w_i, state)
        o_i = jnp.einsum('bhck,bhkv->bhcv', q_i, state) + jnp.einsum('bhij,bhjv->bhiv', a_i, v_new)
        state = state * gl_i[..., None, None] + jnp.einsum('bhck,bhcv->bhkv', k_i, v_new)
        return state, o_i

    _, o = lax.scan(step, jnp.zeros((bsz, H, Dh, Dh), f32), xs)
    o = o.transpose(1, 0, 3, 2, 4).reshape(bsz, seq, H, Dh)
    o = rms_norm(o, out_gain) * jax.nn.silu(z.reshape(bsz, seq, H, Dh).astype(f32))
    return o.reshape(bsz, seq, H * Dh).astype(z.dtype)


def moba_attention(q, k, v, q_gain, k_gain):
    bsz, seq, _ = q.shape
    H, D, BLK, QC = MOBA_HEADS, MOBA_HEAD_DIM, MOBA_BLOCK, MOBA_Q_CHUNK
    n_blocks = -(-seq // BLK)
    seq_p = n_blocks * BLK
    n_qc = seq_p // QC
    topk = min(MOBA_TOPK, n_blocks)
    scale = D ** -0.5
    pos = jnp.arange(seq)
    q = partial_rope(rms_norm(q.reshape(bsz, seq, H, D), q_gain), pos)
    k = partial_rope(rms_norm(k.reshape(bsz, seq, H, D), k_gain), pos)
    v = v.reshape(bsz, seq, H, D)
    pad = [(0, 0), (0, seq_p - seq), (0, 0), (0, 0)]
    q, k, v = (jnp.pad(t, pad).transpose(0, 2, 1, 3) for t in (q, k, v))
    kb = k.reshape(bsz, H, n_blocks, BLK, D)
    vb = v.reshape(bsz, H, n_blocks, BLK, D)
    k_mean = jnp.mean(kb.astype(jnp.float32), axis=3)
    gate = jnp.einsum('bhsd,bhnd->bhsn', q.astype(jnp.float32), k_mean)
    q_block = jnp.arange(seq_p) // BLK
    past = jnp.arange(n_blocks)[None, :] < q_block[:, None]
    gate = jnp.where(past, gate, -jnp.inf)
    _, sel = lax.top_k(gate, topk)

    q_ch = q.reshape(bsz, H, n_qc, QC, D).transpose(0, 2, 1, 3, 4).reshape(bsz * n_qc, H, QC, D)
    sel_ch = sel.reshape(bsz, H, n_qc, QC, topk).transpose(0, 2, 1, 3, 4).reshape(bsz * n_qc, H, QC, topk)
    b_idx = jnp.repeat(jnp.arange(bsz, dtype=jnp.int32), n_qc)
    c_idx = jnp.tile(jnp.arange(n_qc, dtype=jnp.int32), bsz)
    head_ix = jnp.arange(H)[:, None, None]

    def chunk_fn(args):
        qc, selc, bi, ci = args
        kb_b, vb_b = kb[bi], vb[bi]
        k_sel = kb_b[head_ix, selc]
        v_sel = vb_b[head_ix, selc]
        own = (ci * QC) // BLK
        k_own = lax.dynamic_index_in_dim(kb_b, own, axis=1, keepdims=False)
        v_own = lax.dynamic_index_in_dim(vb_b, own, axis=1, keepdims=False)
        valid = jnp.arange(topk) < own
        s_sel = jnp.einsum('hqd,hqtkd->hqtk', qc, k_sel).astype(jnp.float32) * scale
        s_sel = jnp.where(valid[None, None, :, None], s_sel, -jnp.inf)
        q_pos = ci * QC + jnp.arange(QC)
        k_pos = own * BLK + jnp.arange(BLK)
        s_own = jnp.einsum('hqd,hkd->hqk', qc, k_own).astype(jnp.float32) * scale
        s_own = jnp.where(k_pos[None, None, :] <= q_pos[None, :, None], s_own, -jnp.inf)
        p = jax.nn.softmax(jnp.concatenate([s_sel.reshape(H, QC, topk * BLK), s_own], axis=-1), axis=-1)
        p_sel = p[..., :topk * BLK].reshape(H, QC, topk, BLK).astype(v_sel.dtype)
        p_own = p[..., topk * BLK:].astype(v_own.dtype)
        o = jnp.einsum('hqtk,hqtkd->hqd', p_sel, v_sel) + jnp.einsum('hqk,hkd->hqd', p_own, v_own)
        return o.astype(qc.dtype)

    o = lax.map(chunk_fn, (q_ch, sel_ch, b_idx, c_idx))
    o = o.reshape(bsz, n_qc, H, QC, D).transpose(0, 1, 3, 2, 4).reshape(bsz, seq_p, H * D)
    return o[:, :seq]


def setup_inputs(seed: int = 0) -> dict:
    key = jax.random.key(seed)
    ks = jax.random.split(key, 20)
    f32 = jnp.float32

    def normal(k, shape, scale):
        return jax.random.normal(k, shape, f32) * scale

    def gain(k, n):
        return 1.0 + 0.02 * jax.random.normal(k, (DEPTH, n), f32)

    dt = jnp.exp(jax.random.uniform(ks[9], (DEPTH, GDN_HEADS), f32, math.log(1e-3), math.log(1e-1)))
    return {
        "x": normal(ks[0], (BATCH, SEQ, D_MODEL), 1.0),
        "ffn1_norm": gain(ks[1], D_MODEL),
        "ffn1_w_gate": normal(ks[2], (DEPTH, D_MODEL, D_FF), D_MODEL ** -0.5),
        "ffn1_w_up": normal(ks[3], (DEPTH, D_MODEL, D_FF), D_MODEL ** -0.5),
        "ffn1_w_down": normal(ks[4], (DEPTH, D_FF, D_MODEL), D_FF ** -0.5),
        "mix_norm": gain(ks[5], D_MODEL),
        "w_in": normal(ks[6], (DEPTH, D_MODEL, IN_PROJ_DIM), D_MODEL ** -0.5),
        "gdn_conv": normal(ks[7], (DEPTH, CONV_WIDTH, 3 * GDN_WIDTH), CONV_WIDTH ** -0.5),
        "gdn_a_log": jnp.log(jax.random.uniform(ks[8], (DEPTH, GDN_HEADS), f32, 1.0, 16.0)),
        "gdn_dt_bias": dt + jnp.log(-jnp.expm1(-dt)),
        "gdn_out_norm": gain(ks[10], GDN_HEAD_DIM),
        "moba_q_norm": gain(ks[11], MOBA_HEAD_DIM),
        "moba_k_norm": gain(ks[12], MOBA_HEAD_DIM),
        "w_out": normal(ks[13], (DEPTH, MIX_WIDTH, D_MODEL), MIX_WIDTH ** -0.5),
        "ffn2_norm": gain(ks[14], D_MODEL),
        "ffn2_w_gate": normal(ks[15], (DEPTH, D_MODEL, D_FF), D_MODEL ** -0.5),
        "ffn2_w_up": normal(ks[16], (DEPTH, D_MODEL, D_FF), D_MODEL ** -0.5),
        "ffn2_w_down": normal(ks[17], (DEPTH, D_FF, D_MODEL), D_FF ** -0.5),
    }


def reference(x, ffn1_norm, ffn1_w_gate, ffn1_w_up, ffn1_w_down, mix_norm, w_in, gdn_conv,
              gdn_a_log, gdn_dt_bias, gdn_out_norm, moba_q_norm, moba_k_norm, w_out,
              ffn2_norm, ffn2_w_gate, ffn2_w_up, ffn2_w_down):
    for l in range(DEPTH):
        x = x + 0.5 * swiglu_ffn(x, ffn1_norm[l], ffn1_w_gate[l], ffn1_w_up[l], ffn1_w_down[l])
        h = rms_norm(x, mix_norm[l])
        p = h @ w_in[l]
        qkv = jax.nn.silu(causal_dwconv(p[..., :OFF_GDN_QKV], gdn_conv[l]))
        g_q = qkv[..., :GDN_WIDTH]
        g_k = qkv[..., GDN_WIDTH:2 * GDN_WIDTH]
        g_v = qkv[..., 2 * GDN_WIDTH:]
        g_z = p[..., OFF_GDN_QKV:OFF_GDN_Z]
        g_a = p[..., OFF_GDN_Z:OFF_GDN_A]
        g_b = p[..., OFF_GDN_A:OFF_GDN_B]
        o_gdn = gated_deltanet(g_q, g_k, g_v, g_z, g_a, g_b, gdn_a_log[l], gdn_dt_bias[l], gdn_out_norm[l])
        o_moba = moba_attention(p[..., OFF_GDN_B:OFF_MOBA_Q], p[..., OFF_MOBA_Q:OFF_MOBA_K],
                                p[..., OFF_MOBA_K:], moba_q_norm[l], moba_k_norm[l])
        x = x + jnp.concatenate([o_gdn, o_moba], axis=-1) @ w_out[l]
        x = x + 0.5 * swiglu_ffn(x, ffn2_norm[l], ffn2_w_gate[l], ffn2_w_up[l], ffn2_w_down[l])
    return x
```

```python
import functools
import math

import jax
import jax.numpy as jnp
from jax import lax
from jax.experimental import pallas as pl
from jax.experimental.pallas import tpu as pltpu

F32 = jnp.float32
BF16 = jnp.bfloat16

NORM_EPS = 1e-6
GDN_HEADS = 4
GDN_HEAD_DIM = 128
GDN_WIDTH = GDN_HEADS * GDN_HEAD_DIM
GDN_CHUNK = 64
CONV_WIDTH = 4
MOBA_HEADS = 8
MOBA_HEAD_DIM = 64
MOBA_WIDTH = MOBA_HEADS * MOBA_HEAD_DIM
MOBA_BLOCK = 256
MOBA_TOPK = 3
ROPE_THETA = 500000.0
ROPE_DIM = MOBA_HEAD_DIM // 4
LANES = 128
CONV_TAIL = 8
VMEM_LIMIT = 56 * 1024 * 1024

_NT = (((1,), (1,)), ((), ()))
_TN = (((0,), (0,)), ((), ()))


def _bdot(a, b, dims=None):
    a = a.astype(BF16)
    b = b.astype(BF16)
    if dims is None:
        return jnp.dot(a, b, preferred_element_type=F32)
    return lax.dot_general(a, b, dims, preferred_element_type=F32)


def _split(a):
    hi = a.astype(BF16)
    lo = (a - hi.astype(F32)).astype(BF16)
    return hi, lo


def _dot3(a, b, dims=None):
    a_hi, a_lo = _split(a)
    b_hi, b_lo = _split(b)
    return _bdot(a_hi, b_hi, dims) + (_bdot(a_hi, b_lo, dims) + _bdot(a_lo, b_hi, dims))


def _silu(x):
    return x * jax.nn.sigmoid(x)


def _softplus(x):
    return jnp.maximum(x, 0.0) + jnp.log1p(jnp.exp(-jnp.abs(x)))


def _rms_norm(x, g):
    return x * lax.rsqrt(jnp.mean(x * x, axis=-1, keepdims=True) + NORM_EPS) * g


def _ffn_kernel(x_ref, g_ref, wg_ref, wu_ref, wd_ref, o_ref, *, n_split):
    x = x_ref[...]
    h = _rms_norm(x, g_ref[...]).astype(BF16)
    tf = wg_ref.shape[1] // n_split
    acc = None
    for j in range(n_split):
        gate = jnp.dot(h, wg_ref[:, j * tf:(j + 1) * tf], preferred_element_type=F32)
        up = jnp.dot(h, wu_ref[:, j * tf:(j + 1) * tf], preferred_element_type=F32)
        act = (_silu(gate) * up).astype(BF16)
        y = jnp.dot(act, wd_ref[j * tf:(j + 1) * tf, :], preferred_element_type=F32)
        acc = y if acc is None else acc + y
    o_ref[...] = x + 0.5 * acc


def _const_spec(shape):
    return pl.BlockSpec(shape, lambda *_: (0,) * len(shape), pipeline_mode=pl.Buffered(1))


def _ffn(x, norm_g, wg, wu, wd, *, tm=512, n_split=2):
    t, d = x.shape
    ff = wg.shape[1]
    return pl.pallas_call(
        functools.partial(_ffn_kernel, n_split=n_split),
        grid=(t // tm,),
        in_specs=[
            pl.BlockSpec((tm, d), lambda i: (i, 0)),
            _const_spec((1, d)),
            _const_spec((d, ff)),
            _const_spec((d, ff)),
            _const_spec((ff, d)),
        ],
        out_specs=pl.BlockSpec((tm, d), lambda i: (i, 0)),
        out_shape=jax.ShapeDtypeStruct((t, d), F32),
        compiler_params=pltpu.CompilerParams(dimension_semantics=("parallel",), vmem_limit_bytes=VMEM_LIMIT),
        name="ffn",
    )(x, norm_g, wg, wu, wd)


_IN_SPLITS = (3 * GDN_WIDTH, GDN_WIDTH, LANES, MOBA_WIDTH, MOBA_WIDTH, MOBA_WIDTH)


def _inproj_kernel(x_ref, g_ref, w_ref, *out_refs):
    h = _rms_norm(x_ref[...], g_ref[...]).astype(BF16)
    p = jnp.dot(h, w_ref[...], preferred_element_type=F32)
    off = 0
    for ref, width in zip(out_refs, _IN_SPLITS):
        ref[...] = p[:, off:off + width]
        off += width


def _inproj(x, norm_g, w_all, *, tm=512):
    t, d = x.shape
    n = w_all.shape[1]
    return pl.pallas_call(
        _inproj_kernel,
        grid=(t // tm,),
        in_specs=[pl.BlockSpec((tm, d), lambda i: (i, 0)), _const_spec((1, d)), _const_spec((d, n))],
        out_specs=[pl.BlockSpec((tm, w), lambda i: (i, 0)) for w in _IN_SPLITS],
        out_shape=[jax.ShapeDtypeStruct((t, w), F32) for w in _IN_SPLITS],
        compiler_params=pltpu.CompilerParams(dimension_semantics=("parallel",), vmem_limit_bytes=VMEM_LIMIT),
        name="inproj",
    )(x, norm_g, w_all)


def _cumsum_rows(x):
    c = x.shape[0]
    row = lax.broadcasted_iota(jnp.int32, x.shape, 0)
    s = 1
    while s < c:
        x = x + jnp.where(row >= s, pltpu.roll(x, s, axis=0), 0.0)
        s *= 2
    return x


def _gdn_kernel(qkv_ref, z_ref, ab_ref, cw_ref, alog_ref, dtb_ref, og_ref, o_ref, state_sc, tail_sc, *, n_chunks):
    c_len, dh, n_heads, width = GDN_CHUNK, GDN_HEAD_DIM, GDN_HEADS, GDN_WIDTH

    @pl.when(pl.program_id(1) == 0)
    def _reset():
        state_sc[...] = jnp.zeros_like(state_sc)
        tail_sc[...] = jnp.zeros_like(tail_sc)

    ri = lax.broadcasted_iota(jnp.int32, (c_len, c_len), 0)
    ci = lax.broadcasted_iota(jnp.int32, (c_len, c_len), 1)
    causal = ri >= ci
    strict = ri > ci
    eye = jnp.where(ri == ci, 1.0, 0.0).astype(F32)
    cw = cw_ref[...]
    og = og_ref[...]

    def chunk(c, carry):
        t0 = pl.multiple_of(c * c_len, c_len)
        cur = qkv_ref[pl.ds(t0, c_len), :]
        win = jnp.concatenate([tail_sc[...], cur], axis=0)
        conv = None
        for kk in range(CONV_WIDTH):
            lo = CONV_TAIL - (CONV_WIDTH - 1) + kk
            term = win[lo:lo + c_len] * cw[kk:kk + 1]
            conv = term if conv is None else conv + term
        tail_sc[...] = cur[c_len - CONV_TAIL:]
        xc = _silu(conv)
        ab = ab_ref[pl.ds(t0, c_len), :]
        zc = z_ref[pl.ds(t0, c_len), :]
        for h in range(n_heads):
            hs = slice(h * dh, (h + 1) * dh)
            q = xc[:, h * dh:(h + 1) * dh]
            k = xc[:, width + h * dh:width + (h + 1) * dh]
            v = xc[:, 2 * width + h * dh:2 * width + (h + 1) * dh]
            q = q * lax.rsqrt(jnp.sum(q * q, axis=-1, keepdims=True) + NORM_EPS) * (dh ** -0.5)
            k = k * lax.rsqrt(jnp.sum(k * k, axis=-1, keepdims=True) + NORM_EPS)
            a_b = jnp.broadcast_to(ab[:, h:h + 1], (c_len, dh))
            b_b = jnp.broadcast_to(ab[:, n_heads + h:n_heads + h + 1], (c_len, dh))
            beta = jax.nn.sigmoid(b_b)
            g = -jnp.exp(alog_ref[:, hs]) * _softplus(a_b + dtb_ref[:, hs])
            gc = _cumsum_rows(g)
            g_last = gc[c_len - 1:c_len, :]
            e_gc = jnp.exp(gc)
            gc_sq = jnp.concatenate([gc, jnp.zeros((LANES - c_len, dh), F32)], axis=0)
            g_row = gc_sq.T[:c_len, :c_len]
            decay = jnp.exp(jnp.where(causal, gc[:, :c_len] - g_row, -jnp.inf))
            k_beta = k * beta
            lower = jnp.where(strict, _bdot(k_beta, k, _NT) * decay, 0.0)
            intra = _bdot(q, k, _NT) * decay
            pw = -lower
            inv = eye + pw
            sq = 1
            while 2 * sq < c_len:
                pw = _dot3(pw, pw)
                inv = inv + _dot3(inv, pw)
                sq *= 2
            rhs = jnp.concatenate([v * beta, k_beta * e_gc], axis=-1)
            sol = _dot3(inv, rhs)
            u = sol[:, :dh]
            w = sol[:, dh:]
            state = state_sc[h]
            v_new = u - _bdot(w, state)
            o = _bdot(q * e_gc, state) + _bdot(intra, v_new)
            k_dec = k * jnp.exp(g_last - gc)
            state_sc[h] = state * jnp.exp(g_last) + _bdot(k_dec, v_new, _TN)
            zz = zc[:, hs]
            o_ref[pl.ds(t0, c_len), hs] = _rms_norm(o, og) * _silu(zz)
        return carry

    lax.fori_loop(0, n_chunks, chunk, 0)


def _gdn(qkv, z, ab, conv_w, a_log, dt_bias, out_gain, *, batch, seq, seg=1024):
    t = batch * seq
    n_seg = seq // seg
    row_map = lambda b, s: (b * n_seg + s, 0)
    return pl.pallas_call(
        functools.partial(_gdn_kernel, n_chunks=seg // GDN_CHUNK),
        grid=(batch, n_seg),
        in_specs=[
            pl.BlockSpec((seg, 3 * GDN_WIDTH), row_map),
            pl.BlockSpec((seg, GDN_WIDTH), row_map),
            pl.BlockSpec((seg, LANES), row_map),
            _const_spec((CONV_WIDTH, 3 * GDN_WIDTH)),
            _const_spec((1, GDN_WIDTH)),
            _const_spec((1, GDN_WIDTH)),
            _const_spec((1, GDN_HEAD_DIM)),
        ],
        out_specs=pl.BlockSpec((seg, GDN_WIDTH), row_map),
        out_shape=jax.ShapeDtypeStruct((t, GDN_WIDTH), F32),
        scratch_shapes=[
            pltpu.VMEM((GDN_HEADS, GDN_HEAD_DIM, GDN_HEAD_DIM), F32),
            pltpu.VMEM((CONV_TAIL, 3 * GDN_WIDTH), F32),
        ],
        compiler_params=pltpu.CompilerParams(
            dimension_semantics=("arbitrary", "arbitrary"), vmem_limit_bytes=VMEM_LIMIT),
        name="gdn",
    )(qkv, z, ab, conv_w, a_log, dt_bias, out_gain)


_PAIR = LANES // MOBA_HEAD_DIM


def _norm_rope_pair(x, gain, cos_t, sin_t):
    d = MOBA_HEAD_DIM
    lane = lax.broadcasted_iota(jnp.int32, (1, LANES), 1)
    first = lane < d
    x2 = x * x
    s_a = jnp.sum(jnp.where(first, x2, 0.0), axis=-1, keepdims=True)
    s_b = jnp.sum(jnp.where(first, 0.0, x2), axis=-1, keepdims=True)
    ms = jnp.where(first, s_a, s_b) * (1.0 / d)
    xn = x * lax.rsqrt(ms + NORM_EPS) * gain
    half = ROPE_DIM // 2
    partner = jnp.where((lane & (d - 1)) < half,
                        pltpu.roll(xn, LANES - half, axis=1),
                        pltpu.roll(xn, half, axis=1))
    return xn * cos_t + partner * sin_t


def _moba_kernel(q_ref, k_ref, v_ref, qg_ref, kg_ref, cq_ref, sq_ref, ck_ref, sk_ref, o_ref,
                 k_sc, v_sc, km_sc, *, n_blocks):
    blk, d = MOBA_BLOCK, MOBA_HEAD_DIM
    qb = pl.program_id(2)
    neg_inf = -jnp.inf

    @pl.when(qb == 0)
    def _prep_keys():
        km_sc[...] = jnp.zeros_like(km_sc)
        for j in range(n_blocks):
            rows = slice(j * blk, (j + 1) * blk)
            kn = _norm_rope_pair(k_ref[0, rows, :], kg_ref[...], ck_ref[rows, :], sk_ref[rows, :])
            vj = v_ref[0, rows, :]
            for hh in range(_PAIR):
                kh = kn[:, hh * d:(hh + 1) * d]
                k_sc[hh, rows, :] = kh.astype(BF16)
                v_sc[hh, rows, :] = vj[:, hh * d:(hh + 1) * d].astype(BF16)
                km_sc[hh, j:j + 1, :] = jnp.mean(kh, axis=0, keepdims=True)

    qn = _norm_rope_pair(q_ref[0], qg_ref[...], cq_ref[...], sq_ref[...])
    lane = lax.broadcasted_iota(jnp.int32, (blk, LANES), 1)
    lane_f = lane.astype(F32)
    ri = lax.broadcasted_iota(jnp.int32, (blk, blk), 0)
    ci = lax.broadcasted_iota(jnp.int32, (blk, blk), 1)
    own_rows = pl.ds(pl.multiple_of(qb * blk, blk), blk)

    heads = []
    for hh in range(_PAIR):
        qh = qn[:, hh * d:(hh + 1) * d]
        gate = jnp.where(lane < qb, _dot3(qh, km_sc[hh], _NT), neg_inf)
        picks = []
        for r in range(MOBA_TOPK):
            top = jnp.max(gate, axis=-1, keepdims=True)
            idx = jnp.min(jnp.where(gate == top, lane_f, float(LANES)), axis=-1, keepdims=True)
            picks.append(jnp.where(r < qb, idx, -1.0))
            gate = jnp.where(lane_f == idx, neg_inf, gate)
        qs = (qh * (d ** -0.5)).astype(BF16)
        s = jnp.where(ci <= ri, _bdot(qs, k_sc[hh, own_rows, :], _NT), neg_inf)
        m = jnp.max(s, axis=-1, keepdims=True)
        p = jnp.exp(s - m)
        heads.append(dict(qs=qs, picks=picks, m=m, l=jnp.sum(p, axis=-1, keepdims=True),
                          acc=_bdot(p, v_sc[hh, own_rows, :])))

    def past_block(j, carry):
        rows = pl.ds(pl.multiple_of(j * blk, blk), blk)
        jf = j.astype(F32)
        out = []
        for hh in range(_PAIR):
            m, l, acc = carry[hh]
            pk = heads[hh]["picks"]
            chosen = (pk[0] == jf) | (pk[1] == jf) | (pk[2] == jf)
            s = jnp.where(chosen, _bdot(heads[hh]["qs"], k_sc[hh, rows, :], _NT), neg_inf)
            m_new = jnp.maximum(m, jnp.max(s, axis=-1, keepdims=True))
            alpha = jnp.exp(m - m_new)
            p = jnp.exp(s - m_new)
            l = alpha * l + jnp.sum(p, axis=-1, keepdims=True)
            acc = alpha * acc + _bdot(p, v_sc[hh, rows, :])
            out.append((m_new, l, acc))
        return tuple(out)

    final = lax.fori_loop(0, qb, past_block, tuple((hd["m"], hd["l"], hd["acc"]) for hd in heads))
    o_ref[0] = jnp.concatenate([acc / l for (_, l, acc) in final], axis=-1)


def _moba(mq, mk, mv, q_gain, k_gain, cos_t, sin_t, *, batch, seq):
    n_blocks = seq // MOBA_BLOCK
    n_pairs = MOBA_HEADS // _PAIR
    shape3 = (batch, seq, MOBA_WIDTH)
    q_spec = pl.BlockSpec((1, MOBA_BLOCK, LANES), lambda b, p, i: (b, i, p))
    kv_spec = pl.BlockSpec((1, seq, LANES), lambda b, p, i: (b, 0, p))
    out = pl.pallas_call(
        functools.partial(_moba_kernel, n_blocks=n_blocks),
        grid=(batch, n_pairs, n_blocks),
        in_specs=[
            q_spec, kv_spec, kv_spec,
            _const_spec((1, LANES)), _const_spec((1, LANES)),
            pl.BlockSpec((MOBA_BLOCK, LANES), lambda b, p, i: (i, 0)),
            pl.BlockSpec((MOBA_BLOCK, LANES), lambda b, p, i: (i, 0)),
            _const_spec((seq, LANES)), _const_spec((seq, LANES)),
        ],
        out_specs=q_spec,
        out_shape=jax.ShapeDtypeStruct(shape3, F32),
        scratch_shapes=[
            pltpu.VMEM((_PAIR, seq, MOBA_HEAD_DIM), BF16),
            pltpu.VMEM((_PAIR, seq, MOBA_HEAD_DIM), BF16),
            pltpu.VMEM((_PAIR, LANES, MOBA_HEAD_DIM), F32),
        ],
        compiler_params=pltpu.CompilerParams(
            dimension_semantics=("arbitrary", "arbitrary", "arbitrary"), vmem_limit_bytes=VMEM_LIMIT),
        name="moba",
    )(mq.reshape(shape3), mk.reshape(shape3), mv.reshape(shape3), q_gain, k_gain, cos_t, sin_t, cos_t, sin_t)
    return out.reshape(batch * seq, MOBA_WIDTH)


def _rope_tables(seq):
    half = ROPE_DIM // 2
    inv_freq = jnp.power(jnp.float32(ROPE_THETA), -jnp.arange(half, dtype=F32) * 2.0 / ROPE_DIM)
    ang = jnp.arange(seq).astype(F32)[:, None] * inv_freq[None, :]
    cos, sin = jnp.cos(ang), jnp.sin(ang)
    rest = MOBA_HEAD_DIM - ROPE_DIM
    cos_h = jnp.concatenate([cos, cos, jnp.ones((seq, rest), F32)], axis=-1)
    sin_h = jnp.concatenate([-sin, sin, jnp.zeros((seq, rest), F32)], axis=-1)
    return jnp.tile(cos_h, (1, _PAIR)), jnp.tile(sin_h, (1, _PAIR))


def _outproj_kernel(x_ref, a_ref, b_ref, wa_ref, wb_ref, o_ref):
    o_ref[...] = (x_ref[...] + _bdot(a_ref[...], wa_ref[...]) + _bdot(b_ref[...], wb_ref[...]))


def _outproj(x, o_a, o_b, w_a, w_b, *, tm=1024):
    t, d = x.shape
    return pl.pallas_call(
        _outproj_kernel,
        grid=(t // tm,),
        in_specs=[
            pl.BlockSpec((tm, d), lambda i: (i, 0)),
            pl.BlockSpec((tm, o_a.shape[1]), lambda i: (i, 0)),
            pl.BlockSpec((tm, o_b.shape[1]), lambda i: (i, 0)),
            _const_spec(w_a.shape),
            _const_spec(w_b.shape),
        ],
        out_specs=pl.BlockSpec((tm, d), lambda i: (i, 0)),
        out_shape=jax.ShapeDtypeStruct((t, d), F32),
        compiler_params=pltpu.CompilerParams(dimension_semantics=("parallel",), vmem_limit_bytes=VMEM_LIMIT),
        name="outproj",
    )(x, o_a, o_b, w_a, w_b)


def kernel(x, ffn1_norm, ffn1_w_gate, ffn1_w_up, ffn1_w_down, mix_norm, w_in, gdn_conv, gdn_a_log, gdn_dt_bias,
           gdn_out_norm, moba_q_norm, moba_k_norm, w_out, ffn2_norm, ffn2_w_gate, ffn2_w_up, ffn2_w_down):
    batch, seq, d_model = x.shape
    depth = w_in.shape[0]
    off_z = 3 * GDN_WIDTH
    off_ab = 4 * GDN_WIDTH
    off_moba = off_ab + 2 * GDN_HEADS
    cos_t, sin_t = _rope_tables(seq)
    xf = x.reshape(batch * seq, d_model)
    for l in range(depth):
        xf = _ffn(xf, ffn1_norm[l][None], ffn1_w_gate[l].astype(BF16), ffn1_w_up[l].astype(BF16),
                  ffn1_w_down[l].astype(BF16))
        w = w_in[l]
        w_ab = jnp.pad(w[:, off_ab:off_moba], ((0, 0), (0, LANES - 2 * GDN_HEADS)))
        w_all = jnp.concatenate([w[:, :off_ab], w_ab, w[:, off_moba:]], axis=1).astype(BF16)
        qkv, z, ab, mq, mk, mv = _inproj(xf, mix_norm[l][None], w_all)
        a_log = jnp.repeat(gdn_a_log[l].astype(F32), GDN_HEAD_DIM)[None]
        dt_b = jnp.repeat(gdn_dt_bias[l].astype(F32), GDN_HEAD_DIM)[None]
        o_gdn = _gdn(qkv, z, ab, gdn_conv[l], a_log, dt_b, gdn_out_norm[l][None], batch=batch, seq=seq)
        o_moba = _moba(mq, mk, mv, jnp.tile(moba_q_norm[l], _PAIR)[None], jnp.tile(moba_k_norm[l], _PAIR)[None],
                       cos_t, sin_t, batch=batch, seq=seq)
        wo = w_out[l].astype(BF16)
        xf = _outproj(xf, o_gdn, o_moba, wo[:GDN_WIDTH], wo[GDN_WIDTH:])
        xf = _ffn(xf, ffn2_norm[l][None], ffn2_w_gate[l].astype(BF16), ffn2_w_up[l].astype(BF16),
                  ffn2_w_down[l].astype(BF16))
    return xf.reshape(batch, seq, d_model)
```

```python
import functools
import math

import jax
import jax.numpy as jnp
from jax import lax
from jax.experimental import pallas as pl
from jax.experimental.pallas import tpu as pltpu

F32 = jnp.float32
BF16 = jnp.bfloat16

NORM_EPS = 1e-6
GDN_HEADS = 4
GDN_HEAD_DIM = 128
GDN_WIDTH = GDN_HEADS * GDN_HEAD_DIM
GDN_CHUNK = 64
CONV_WIDTH = 4
MOBA_HEADS = 8
MOBA_HEAD_DIM = 64
MOBA_WIDTH = MOBA_HEADS * MOBA_HEAD_DIM
MOBA_BLOCK = 256
MOBA_TOPK = 3
ROPE_THETA = 500000.0
ROPE_DIM = MOBA_HEAD_DIM // 4
LANES = 128
CONV_TAIL = 8
VMEM_LIMIT = 56 * 1024 * 1024

_NT = (((1,), (1,)), ((), ()))
_TN = (((0,), (0,)), ((), ()))


def _bdot(a, b, dims=None):
    a = a.astype(BF16)
    b = b.astype(BF16)
    if dims is None:
        return jnp.dot(a, b, preferred_element_type=F32)
    return lax.dot_general(a, b, dims, preferred_element_type=F32)


def _split(a):
    hi = a.astype(BF16)
    lo = (a - hi.astype(F32)).astype(BF16)
    return hi, lo


def _dot3(a, b, dims=None):
    a_hi, a_lo = _split(a)
    b_hi, b_lo = _split(b)
    return _bdot(a_hi, b_hi, dims) + (_bdot(a_hi, b_lo, dims) + _bdot(a_lo, b_hi, dims))


def _silu(x):
    return x * jax.nn.sigmoid(x)


def _softplus(x):
    return jnp.maximum(x, 0.0) + jnp.log1p(jnp.exp(-jnp.abs(x)))


def _rms_norm(x, g):
    return x * lax.rsqrt(jnp.mean(x * x, axis=-1, keepdims=True) + NORM_EPS) * g


def _ffn_kernel(x_ref, g_ref, wg_ref, wu_ref, wd_ref, o_ref, *, n_split):
    x = x_ref[...]
    h = _rms_norm(x, g_ref[...]).astype(BF16)
    tf = wg_ref.shape[1] // n_split
    acc = None
    for j in range(n_split):
        gate = jnp.dot(h, wg_ref[:, j * tf:(j + 1) * tf], preferred_element_type=F32)
        up = jnp.dot(h, wu_ref[:, j * tf:(j + 1) * tf], preferred_element_type=F32)
        act = (_silu(gate) * up).astype(BF16)
        y = jnp.dot(act, wd_ref[j * tf:(j + 1) * tf, :], preferred_element_type=F32)
        acc = y if acc is None else acc + y
    o_ref[...] = x + 0.5 * acc


def _const_spec(shape):
    return pl.BlockSpec(shape, lambda *_: (0,) * len(shape), pipeline_mode=pl.Buffered(1))


def _ffn(x, norm_g, wg, wu, wd, *, tm=512, n_split=2):
    t, d = x.shape
    ff = wg.shape[1]
    return pl.pallas_call(
        functools.partial(_ffn_kernel, n_split=n_split),
        grid=(t // tm,),
        in_specs=[
            pl.BlockSpec((tm, d), lambda i: (i, 0)),
            _const_spec((1, d)),
            _const_spec((d, ff)),
            _const_spec((d, ff)),
            _const_spec((ff, d)),
        ],
        out_specs=pl.BlockSpec((tm, d), lambda i: (i, 0)),
        out_shape=jax.ShapeDtypeStruct((t, d), F32),
        compiler_params=pltpu.CompilerParams(dimension_semantics=("parallel",), vmem_limit_bytes=VMEM_LIMIT),
        name="ffn",
    )(x, norm_g, wg, wu, wd)


_IN_SPLITS = (3 * GDN_WIDTH, GDN_WIDTH, LANES, MOBA_WIDTH, MOBA_WIDTH, MOBA_WIDTH)


def _inproj_kernel(x_ref, g_ref, w_ref, *out_refs):
    h = _rms_norm(x_ref[...], g_ref[...]).astype(BF16)
    p = jnp.dot(h, w_ref[...], preferred_element_type=F32)
    off = 0
    for ref, width in zip(out_refs, _IN_SPLITS):
        ref[...] = p[:, off:off + width]
        off += width


def _inproj(x, norm_g, w_all, *, tm=512):
    t, d = x.shape
    n = w_all.shape[1]
    return pl.pallas_call(
        _inproj_kernel,
        grid=(t // tm,),
        in_specs=[pl.BlockSpec((tm, d), lambda i: (i, 0)), _const_spec((1, d)), _const_spec((d, n))],
        out_specs=[pl.BlockSpec((tm, w), lambda i: (i, 0)) for w in _IN_SPLITS],
        out_shape=[jax.ShapeDtypeStruct((t, w), F32) for w in _IN_SPLITS],
        compiler_params=pltpu.CompilerParams(dimension_semantics=("parallel",), vmem_limit_bytes=VMEM_LIMIT),
        name="inproj",
    )(x, norm_g, w_all)


def _cumsum_rows(x):
    c = x.shape[0]
    row = lax.broadcasted_iota(jnp.int32, x.shape, 0)
    s = 1
    while s < c:
        x = x + jnp.where(row >= s, pltpu.roll(x, s, axis=0), 0.0)
        s *= 2
    return x


def _gdn_kernel(qkv_ref, z_ref, ab_ref, cw_ref, alog_ref, dtb_ref, og_ref, o_ref, state_sc, tail_sc, *, n_chunks):
    c_len, dh, n_heads, width = GDN_CHUNK, GDN_HEAD_DIM, GDN_HEADS, GDN_WIDTH

    @pl.when(pl.program_id(1) == 0)
    def _reset():
        state_sc[...] = jnp.zeros_like(state_sc)
        tail_sc[...] = jnp.zeros_like(tail_sc)

    ri = lax.broadcasted_iota(jnp.int32, (c_len, c_len), 0)
    ci = lax.broadcasted_iota(jnp.int32, (c_len, c_len), 1)
    causal = ri >= ci
    strict = ri > ci
    eye = jnp.where(ri == ci, 1.0, 0.0).astype(F32)
    cw = cw_ref[...]
    og = og_ref[...]

    def chunk(c, carry):
        t0 = pl.multiple_of(c * c_len, c_len)
        cur = qkv_ref[pl.ds(t0, c_len), :]
        win = jnp.concatenate([tail_sc[...], cur], axis=0)
        conv = None
        for kk in range(CONV_WIDTH):
            lo = CONV_TAIL - (CONV_WIDTH - 1) + kk
            term = win[lo:lo + c_len] * cw[kk:kk + 1]
            conv = term if conv is None else conv + term
        tail_sc[...] = cur[c_len - CONV_TAIL:]
        xc = _silu(conv)
        ab = ab_ref[pl.ds(t0, c_len), :]
        zc = z_ref[pl.ds(t0, c_len), :]
        for h in range(n_heads):
            hs = slice(h * dh, (h + 1) * dh)
            q = xc[:, h * dh:(h + 1) * dh]
            k = xc[:, width + h * dh:width + (h + 1) * dh]
            v = xc[:, 2 * width + h * dh:2 * width + (h + 1) * dh]
            q = q * lax.rsqrt(jnp.sum(q * q, axis=-1, keepdims=True) + NORM_EPS) * (dh ** -0.5)
            k = k * lax.rsqrt(jnp.sum(k * k, axis=-1, keepdims=True) + NORM_EPS)
            a_b = jnp.broadcast_to(ab[:, h:h + 1], (c_len, dh))
            b_b = jnp.broadcast_to(ab[:, n_heads + h:n_heads + h + 1], (c_len, dh))
            beta = jax.nn.sigmoid(b_b)
            g = -jnp.exp(alog_ref[:, hs]) * _softplus(a_b + dtb_ref[:, hs])
            gc = _cumsum_rows(g)
            g_last = gc[c_len - 1:c_len, :]
            e_gc = jnp.exp(gc)
            gc_sq = jnp.concatenate([gc, jnp.zeros((LANES - c_len, dh), F32)], axis=0)
            g_row = gc_sq.T[:c_len, :c_len]
            decay = jnp.exp(jnp.where(causal, gc[:, :c_len] - g_row, -jnp.inf))
            k_beta = k * beta
            lower = jnp.where(strict, _bdot(k_beta, k, _NT) * decay, 0.0)
            intra = _bdot(q, k, _NT) * decay
            pw = -lower
            inv = eye + pw
            sq = 1
            while 2 * sq < c_len:
                pw = _dot3(pw, pw)
                inv = inv + _dot3(inv, pw)
                sq *= 2
            rhs = jnp.concatenate([v * beta, k_beta * e_gc], axis=-1)
            sol = _dot3(inv, rhs)
            u = sol[:, :dh]
            w = sol[:, dh:]
            state = state_sc[h]
            v_new = u - _bdot(w, state)
            o = _bdot(q * e_gc, state) + _bdot(intra, v_new)
            k_dec = k * jnp.exp(g_last - gc)
            state_sc[h] = state * jnp.exp(g_last) + _bdot(k_dec, v_new, _TN)
            zz = zc[:, hs]
            o_ref[pl.ds(t0, c_len), hs] = _rms_norm(o, og) * _silu(zz)
        return carry

    lax.fori_loop(0, n_chunks, chunk, 0)


def _gdn(qkv, z, ab, conv_w, a_log, dt_bias, out_gain, *, batch, seq, seg=1024):
    t = batch * seq
    n_seg = seq // seg
    row_map = lambda b, s: (b * n_seg + s, 0)
    return pl.pallas_call(
        functools.partial(_gdn_kernel, n_chunks=seg // GDN_CHUNK),
        grid=(batch, n_seg),
        in_specs=[
            pl.BlockSpec((seg, 3 * GDN_WIDTH), row_map),
            pl.BlockSpec((seg, GDN_WIDTH), row_map),
            pl.BlockSpec((seg, LANES), row_map),
            _const_spec((CONV_WIDTH, 3 * GDN_WIDTH)),
            _const_spec((1, GDN_WIDTH)),
            _const_spec((1, GDN_WIDTH)),
            _const_spec((1, GDN_HEAD_DIM)),
        ],
        out_specs=pl.BlockSpec((seg, GDN_WIDTH), row_map),
        out_shape=jax.ShapeDtypeStruct((t, GDN_WIDTH), F32),
        scratch_shapes=[
            pltpu.VMEM((GDN_HEADS, GDN_HEAD_DIM, GDN_HEAD_DIM), F32),
            pltpu.VMEM((CONV_TAIL, 3 * GDN_WIDTH), F32),
        ],
        compiler_params=pltpu.CompilerParams(
            dimension_semantics=("arbitrary", "arbitrary"), vmem_limit_bytes=VMEM_LIMIT),
        name="gdn",
    )(qkv, z, ab, conv_w, a_log, dt_bias, out_gain)


_PAIR = LANES // MOBA_HEAD_DIM
_MOBA_STEP_HEADS = 4


def _norm_rope_pair(x, gain, cos_t, sin_t):
    d = MOBA_HEAD_DIM
    lane = lax.broadcasted_iota(jnp.int32, (1, LANES), 1)
    first = lane < d
    x2 = x * x
    s_a = jnp.sum(jnp.where(first, x2, 0.0), axis=-1, keepdims=True)
    s_b = jnp.sum(jnp.where(first, 0.0, x2), axis=-1, keepdims=True)
    ms = jnp.where(first, s_a, s_b) * (1.0 / d)
    xn = x * lax.rsqrt(ms + NORM_EPS) * gain
    half = ROPE_DIM // 2
    partner = jnp.where((lane & (d - 1)) < half,
                        pltpu.roll(xn, LANES - half, axis=1),
                        pltpu.roll(xn, half, axis=1))
    return xn * cos_t + partner * sin_t


def _moba_kernel(q_ref, k_ref, v_ref, qg_ref, kg_ref, cq_ref, sq_ref, ck_ref, sk_ref, o_ref,
                 k_sc, vt_sc, km_sc, p_sc, *, n_blocks, n_heads):
    blk, d = MOBA_BLOCK, MOBA_HEAD_DIM
    qb = pl.program_id(2)
    neg_inf = -jnp.inf
    heads = range(n_heads)

    def slab(ref_rows, pr):
        return ref_rows[:, pr * LANES:(pr + 1) * LANES]

    @pl.when(qb == 0)
    def _prep_keys():
        for j in range(n_blocks):
            rows = slice(j * blk, (j + 1) * blk)
            for pr in range(n_heads // _PAIR):
                kn = _norm_rope_pair(slab(k_ref[0, rows, :], pr), kg_ref[...], ck_ref[rows, :], sk_ref[rows, :])
                vt = slab(v_ref[0, rows, :], pr).T
                for h2 in range(_PAIR):
                    hh = pr * _PAIR + h2
                    kh = kn[:, h2 * d:(h2 + 1) * d]
                    km_sc[hh, j:j + 1, :] = jnp.mean(kh, axis=0, keepdims=True)
                    k_sc[hh, j] = kh.astype(BF16)
                    vt_sc[hh, j] = vt[h2 * d:(h2 + 1) * d, :].astype(BF16)

    blk_id = lax.broadcasted_iota(jnp.int32, (n_blocks, blk), 0)
    blk_f = blk_id.astype(F32)
    key_i = lax.broadcasted_iota(jnp.int32, (blk, blk), 0)
    qry_i = lax.broadcasted_iota(jnp.int32, (blk, blk), 1)

    q_scaled, picks, stats0 = [], [], []
    for pr in range(n_heads // _PAIR):
        qt_pair = _norm_rope_pair(slab(q_ref[0], pr), qg_ref[...], cq_ref[...], sq_ref[...]).T
        for h2 in range(_PAIR):
            hh = pr * _PAIR + h2
            qt = qt_pair[h2 * d:(h2 + 1) * d, :]
            gate = jnp.where(blk_id < qb, _dot3(km_sc[hh], qt), neg_inf)
            pk = []
            for r in range(MOBA_TOPK):
                top = jnp.max(gate, axis=0, keepdims=True)
                idx = jnp.min(jnp.where(gate == top, blk_f, float(n_blocks)), axis=0, keepdims=True)
                pk.append(jnp.where(r < qb, idx, -1.0))
                gate = jnp.where(blk_f == idx, neg_inf, gate)
            picks.append(pk)
            qs = (qt * (d ** -0.5)).astype(BF16)
            q_scaled.append(qs)
            s_t = jnp.where(key_i <= qry_i, _bdot(k_sc[hh, qb], qs), neg_inf)
            m = jnp.max(s_t, axis=0, keepdims=True)
            p_t = jnp.exp(s_t - m)
            p_sc[hh] = p_t.astype(BF16)
            stats0.append((m, jnp.sum(p_t, axis=0, keepdims=True), jnp.zeros((d, blk), F32)))

    def past_block(j, carry):
        jf = j.astype(F32)
        pending = jnp.where(j == 0, qb, j - 1)
        s_all = [_bdot(k_sc[hh, j], q_scaled[hh]) for hh in heads]
        pv_all = [_bdot(vt_sc[hh, pending], p_sc[hh]) for hh in heads]
        out = []
        for hh in heads:
            m, l, acc = carry[hh]
            pk = picks[hh]
            chosen = (pk[0] == jf) | (pk[1] == jf) | (pk[2] == jf)
            s_t = jnp.where(chosen, s_all[hh], neg_inf)
            m_new = jnp.maximum(m, jnp.max(s_t, axis=0, keepdims=True))
            alpha = jnp.exp(m - m_new)
            p_t = jnp.exp(s_t - m_new)
            p_sc[hh] = p_t.astype(BF16)
            out.append((m_new, alpha * l + jnp.sum(p_t, axis=0, keepdims=True), alpha * (acc + pv_all[hh])))
        return tuple(out)

    final = lax.fori_loop(0, qb, past_block, tuple(stats0))
    last = jnp.maximum(qb - 1, 0)
    o_t = jnp.concatenate([(acc + _bdot(vt_sc[hh, last], p_sc[hh])) / l
                           for hh, (_, l, acc) in enumerate(final)], axis=0)
    o_ref[0] = o_t.T


def _moba(mq, mk, mv, q_gain, k_gain, cos_t, sin_t, *, batch, seq):
    n_blocks = seq // MOBA_BLOCK
    n_heads = _MOBA_STEP_HEADS
    cols = n_heads * MOBA_HEAD_DIM
    shape3 = (batch, seq, MOBA_WIDTH)
    q_spec = pl.BlockSpec((1, MOBA_BLOCK, cols), lambda b, g, i: (b, i, g))
    kv_spec = pl.BlockSpec((1, seq, cols), lambda b, g, i: (b, 0, g))
    out = pl.pallas_call(
        functools.partial(_moba_kernel, n_blocks=n_blocks, n_heads=n_heads),
        grid=(batch, MOBA_HEADS // n_heads, n_blocks),
        in_specs=[
            q_spec, kv_spec, kv_spec,
            _const_spec((1, LANES)), _const_spec((1, LANES)),
            pl.BlockSpec((MOBA_BLOCK, LANES), lambda b, g, i: (i, 0)),
            pl.BlockSpec((MOBA_BLOCK, LANES), lambda b, g, i: (i, 0)),
            _const_spec((seq, LANES)), _const_spec((seq, LANES)),
        ],
        out_specs=q_spec,
        out_shape=jax.ShapeDtypeStruct(shape3, F32),
        scratch_shapes=[
            pltpu.VMEM((n_heads, n_blocks, MOBA_BLOCK, MOBA_HEAD_DIM), BF16),
            pltpu.VMEM((n_heads, n_blocks, MOBA_HEAD_DIM, MOBA_BLOCK), BF16),
            pltpu.VMEM((n_heads, n_blocks, MOBA_HEAD_DIM), F32),
            pltpu.VMEM((n_heads, MOBA_BLOCK, MOBA_BLOCK), BF16),
        ],
        compiler_params=pltpu.CompilerParams(
            dimension_semantics=("arbitrary", "arbitrary", "arbitrary"), vmem_limit_bytes=VMEM_LIMIT),
        name="moba",
    )(mq.reshape(shape3), mk.reshape(shape3), mv.reshape(shape3), q_gain, k_gain, cos_t, sin_t, cos_t, sin_t)
    return out.reshape(batch * seq, MOBA_WIDTH)


def _rope_tables(seq):
    half = ROPE_DIM // 2
    inv_freq = jnp.power(jnp.float32(ROPE_THETA), -jnp.arange(half, dtype=F32) * 2.0 / ROPE_DIM)
    ang = jnp.arange(seq).astype(F32)[:, None] * inv_freq[None, :]
    cos, sin = jnp.cos(ang), jnp.sin(ang)
    rest = MOBA_HEAD_DIM - ROPE_DIM
    cos_h = jnp.concatenate([cos, cos, jnp.ones((seq, rest), F32)], axis=-1)
    sin_h = jnp.concatenate([-sin, sin, jnp.zeros((seq, rest), F32)], axis=-1)
    return jnp.tile(cos_h, (1, _PAIR)), jnp.tile(sin_h, (1, _PAIR))


def _outproj_kernel(x_ref, a_ref, b_ref, wa_ref, wb_ref, o_ref):
    o_ref[...] = (x_ref[...] + _bdot(a_ref[...], wa_ref[...]) + _bdot(b_ref[...], wb_ref[...]))


def _outproj(x, o_a, o_b, w_a, w_b, *, tm=1024):
    t, d = x.shape
    return pl.pallas_call(
        _outproj_kernel,
        grid=(t // tm,),
        in_specs=[
            pl.BlockSpec((tm, d), lambda i: (i, 0)),
            pl.BlockSpec((tm, o_a.shape[1]), lambda i: (i, 0)),
            pl.BlockSpec((tm, o_b.shape[1]), lambda i: (i, 0)),
            _const_spec(w_a.shape),
            _const_spec(w_b.shape),
        ],
        out_specs=pl.BlockSpec((tm, d), lambda i: (i, 0)),
        out_shape=jax.ShapeDtypeStruct((t, d), F32),
        compiler_params=pltpu.CompilerParams(dimension_semantics=("parallel",), vmem_limit_bytes=VMEM_LIMIT),
        name="outproj",
    )(x, o_a, o_b, w_a, w_b)


def kernel(x, ffn1_norm, ffn1_w_gate, ffn1_w_up, ffn1_w_down, mix_norm, w_in, gdn_conv, gdn_a_log, gdn_dt_bias,
           gdn_out_norm, moba_q_norm, moba_k_norm, w_out, ffn2_norm, ffn2_w_gate, ffn2_w_up, ffn2_w_down):
    batch, seq, d_model = x.shape
    depth = w_in.shape[0]
    off_z = 3 * GDN_WIDTH
    off_ab = 4 * GDN_WIDTH
    off_moba = off_ab + 2 * GDN_HEADS
    cos_t, sin_t = _rope_tables(seq)
    xf = x.reshape(batch * seq, d_model)
    for l in range(depth):
        xf = _ffn(xf, ffn1_norm[l][None], ffn1_w_gate[l].astype(BF16), ffn1_w_up[l].astype(BF16),
                  ffn1_w_down[l].astype(BF16))
        w = w_in[l]
        w_ab = jnp.pad(w[:, off_ab:off_moba], ((0, 0), (0, LANES - 2 * GDN_HEADS)))
        w_all = jnp.concatenate([w[:, :off_ab], w_ab, w[:, off_moba:]], axis=1).astype(BF16)
        qkv, z, ab, mq, mk, mv = _inproj(xf, mix_norm[l][None], w_all)
        a_log = jnp.repeat(gdn_a_log[l].astype(F32), GDN_HEAD_DIM)[None]
        dt_b = jnp.repeat(gdn_dt_bias[l].astype(F32), GDN_HEAD_DIM)[None]
        o_gdn = _gdn(qkv, z, ab, gdn_conv[l], a_log, dt_b, gdn_out_norm[l][None], batch=batch, seq=seq)
        o_moba = _moba(mq, mk, mv, jnp.tile(moba_q_norm[l], _PAIR)[None], jnp.tile(moba_k_norm[l], _PAIR)[None],
                       cos_t, sin_t, batch=batch, seq=seq)
        wo = w_out[l].astype(BF16)
        xf = _outproj(xf, o_gdn, o_moba, wo[:GDN_WIDTH], wo[GDN_WIDTH:])
        xf = _ffn(xf, ffn2_norm[l][None], ffn2_w_gate[l].astype(BF16), ffn2_w_up[l].astype(BF16),
                  ffn2_w_down[l].astype(BF16))
    return xf.reshape(batch, seq, d_model)
```

```python
import functools
import math

import jax
import jax.numpy as jnp
from jax import lax
from jax.experimental import pallas as pl
from jax.experimental.pallas import tpu as pltpu

F32 = jnp.float32
BF16 = jnp.bfloat16

NORM_EPS = 1e-6
GDN_HEADS = 4
GDN_HEAD_DIM = 128
GDN_WIDTH = GDN_HEADS * GDN_HEAD_DIM
GDN_CHUNK = 64
CONV_WIDTH = 4
MOBA_HEADS = 8
MOBA_HEAD_DIM = 64
MOBA_WIDTH = MOBA_HEADS * MOBA_HEAD_DIM
MOBA_BLOCK = 256
MOBA_TOPK = 3
ROPE_THETA = 500000.0
ROPE_DIM = MOBA_HEAD_DIM // 4
LANES = 128
CONV_TAIL = 8
VMEM_LIMIT = 56 * 1024 * 1024

_NT = (((1,), (1,)), ((), ()))
_TN = (((0,), (0,)), ((), ()))


def _bdot(a, b, dims=None):
    a = a.astype(BF16)
    b = b.astype(BF16)
    if dims is None:
        return jnp.dot(a, b, preferred_element_type=F32)
    return lax.dot_general(a, b, dims, preferred_element_type=F32)


def _split(a):
    hi = a.astype(BF16)
    lo = (a - hi.astype(F32)).astype(BF16)
    return hi, lo


def _dot3(a, b, dims=None):
    a_hi, a_lo = _split(a)
    b_hi, b_lo = _split(b)
    return _bdot(a_hi, b_hi, dims) + (_bdot(a_hi, b_lo, dims) + _bdot(a_lo, b_hi, dims))


def _silu(x):
    return x * jax.nn.sigmoid(x)


def _softplus(x):
    return jnp.maximum(x, 0.0) + jnp.log1p(jnp.exp(-jnp.abs(x)))


def _rms_norm(x, g):
    return x * lax.rsqrt(jnp.mean(x * x, axis=-1, keepdims=True) + NORM_EPS) * g


def _ffn_kernel(x_ref, g_ref, wg_ref, wu_ref, wd_ref, o_ref, *, n_split):
    x = x_ref[...]
    h = _rms_norm(x, g_ref[...]).astype(BF16)
    tf = wg_ref.shape[1] // n_split
    acc = None
    for j in range(n_split):
        gate = jnp.dot(h, wg_ref[:, j * tf:(j + 1) * tf], preferred_element_type=F32)
        up = jnp.dot(h, wu_ref[:, j * tf:(j + 1) * tf], preferred_element_type=F32)
        act = (_silu(gate) * up).astype(BF16)
        y = jnp.dot(act, wd_ref[j * tf:(j + 1) * tf, :], preferred_element_type=F32)
        acc = y if acc is None else acc + y
    o_ref[...] = x + 0.5 * acc


def _const_spec(shape):
    return pl.BlockSpec(shape, lambda *_: (0,) * len(shape), pipeline_mode=pl.Buffered(1))


def _ffn(x, norm_g, wg, wu, wd, *, tm=512, n_split=2):
    t, d = x.shape
    ff = wg.shape[1]
    return pl.pallas_call(
        functools.partial(_ffn_kernel, n_split=n_split),
        grid=(t // tm,),
        in_specs=[
            pl.BlockSpec((tm, d), lambda i: (i, 0)),
            _const_spec((1, d)),
            _const_spec((d, ff)),
            _const_spec((d, ff)),
            _const_spec((ff, d)),
        ],
        out_specs=pl.BlockSpec((tm, d), lambda i: (i, 0)),
        out_shape=jax.ShapeDtypeStruct((t, d), F32),
        compiler_params=pltpu.CompilerParams(dimension_semantics=("parallel",), vmem_limit_bytes=VMEM_LIMIT),
        name="ffn",
    )(x, norm_g, wg, wu, wd)


_IN_SPLITS = (3 * GDN_WIDTH, GDN_WIDTH, LANES, MOBA_WIDTH, MOBA_WIDTH, MOBA_WIDTH)


def _inproj_kernel(x_ref, g_ref, w_ref, *out_refs):
    h = _rms_norm(x_ref[...], g_ref[...]).astype(BF16)
    p = jnp.dot(h, w_ref[...], preferred_element_type=F32)
    off = 0
    for ref, width in zip(out_refs, _IN_SPLITS):
        ref[...] = p[:, off:off + width]
        off += width


def _inproj(x, norm_g, w_all, *, tm=512):
    t, d = x.shape
    n = w_all.shape[1]
    return pl.pallas_call(
        _inproj_kernel,
        grid=(t // tm,),
        in_specs=[pl.BlockSpec((tm, d), lambda i: (i, 0)), _const_spec((1, d)), _const_spec((d, n))],
        out_specs=[pl.BlockSpec((tm, w), lambda i: (i, 0)) for w in _IN_SPLITS],
        out_shape=[jax.ShapeDtypeStruct((t, w), F32) for w in _IN_SPLITS],
        compiler_params=pltpu.CompilerParams(dimension_semantics=("parallel",), vmem_limit_bytes=VMEM_LIMIT),
        name="inproj",
    )(x, norm_g, w_all)


_GDN_GROUP = 4
_BNT = (((2,), (2,)), ((0,), (0,)))
_BNN = (((2,), (1,)), ((0,), (0,)))
_BTN = (((1,), (1,)), ((0,), (0,)))


def _chunk_cumsum(x, c_len):
    pos = lax.broadcasted_iota(jnp.int32, x.shape, 0) & (c_len - 1)
    s = 1
    while s < c_len:
        x = x + jnp.where(pos >= s, pltpu.roll(x, s, axis=0), 0.0)
        s *= 2
    return x


def _unit_lower_inverse(lower, base=16):
    c = lower.shape[-1]
    ri = lax.broadcasted_iota(jnp.int32, (1, c, c), 1)
    ci = lax.broadcasted_iota(jnp.int32, (1, c, c), 2)
    same = lambda size: (ri // size) == (ci // size)
    pw = jnp.where(same(base), -lower, 0.0)
    inv = jnp.where(ri == ci, 1.0, 0.0) + pw
    sq = 1
    while 2 * sq < base:
        pw = _bdot(pw, pw, _BNN)
        inv = inv + _bdot(inv, pw, _BNN)
        sq *= 2
    size = base
    while size < c:
        off = jnp.where(same(size), 0.0, jnp.where(same(2 * size), lower, 0.0))
        inv = inv - _bdot(_bdot(inv, off, _BNN), inv, _BNN)
        size *= 2
    return inv


def _gdn_kernel(qkv_ref, z_ref, ab_ref, cw_ref, alog_ref, dtb_ref, og_ref, o_ref, state_sc, xpad_sc, xc_sc, *, n_groups):
    c_len, dh, n_heads, width, grp = GDN_CHUNK, GDN_HEAD_DIM, GDN_HEADS, GDN_WIDTH, _GDN_GROUP
    rows = grp * c_len
    nb = grp * n_heads

    seg = qkv_ref.shape[0]

    @pl.when(pl.program_id(1) == 0)
    def _reset():
        state_sc[...] = jnp.zeros_like(state_sc)
        xpad_sc[0:CONV_TAIL, :] = jnp.zeros((CONV_TAIL, xpad_sc.shape[1]), F32)

    @pl.when(pl.program_id(1) != 0)
    def _carry_history():
        xpad_sc[0:CONV_TAIL, :] = xpad_sc[seg:seg + CONV_TAIL, :]

    xpad_sc[CONV_TAIL:CONV_TAIL + seg, :] = qkv_ref[...]
    cw = cw_ref[...]
    for t in range(seg // rows):
        win = xpad_sc[t * rows:t * rows + CONV_TAIL + rows, :]
        conv = win * cw[CONV_WIDTH - 1:CONV_WIDTH]
        for back in range(1, CONV_WIDTH):
            conv = conv + pltpu.roll(win, back, axis=0) * cw[CONV_WIDTH - 1 - back:CONV_WIDTH - back]
        xc_sc[t * rows:(t + 1) * rows, :] = _silu(conv[CONV_TAIL:])

    ri = lax.broadcasted_iota(jnp.int32, (1, c_len, c_len), 1)
    ci = lax.broadcasted_iota(jnp.int32, (1, c_len, c_len), 2)
    causal = ri >= ci
    strict = ri > ci
    og = og_ref[...]

    def tiles(x, off):
        return jnp.stack([x[g * c_len:(g + 1) * c_len, off + h * dh:off + (h + 1) * dh]
                          for g in range(grp) for h in range(n_heads)])

    def lane_tiles(x, lane0):
        return jnp.stack([jnp.broadcast_to(x[g * c_len:(g + 1) * c_len, lane0 + h:lane0 + h + 1], (c_len, dh))
                          for g in range(grp) for h in range(n_heads)])

    def group(gi, carry):
        r0 = pl.multiple_of(gi * rows, rows)
        xc = xc_sc[pl.ds(r0, rows), :]
        ab = ab_ref[pl.ds(r0, rows), :]
        g_tok = -jnp.exp(alog_ref[...]) * _softplus(ab + dtb_ref[...])
        gc_tok = _chunk_cumsum(g_tok, c_len)
        gc = lane_tiles(gc_tok, 0)
        gc_t = gc_tok.T
        gc_row = jnp.stack([jnp.broadcast_to(gc_t[h:h + 1, g * c_len:(g + 1) * c_len], (c_len, c_len))
                            for g in range(grp) for h in range(n_heads)])
        beta = lane_tiles(jax.nn.sigmoid(ab), n_heads)
        q = tiles(xc, 0)
        k = tiles(xc, width)
        v = tiles(xc, 2 * width)
        q = q * lax.rsqrt(jnp.sum(q * q, axis=-1, keepdims=True) + NORM_EPS) * (dh ** -0.5)
        k = k * lax.rsqrt(jnp.sum(k * k, axis=-1, keepdims=True) + NORM_EPS)
        g_last = gc[:, c_len - 1:c_len, :]
        e_gc = jnp.exp(gc)
        decay = jnp.exp(jnp.where(causal, gc[:, :, :c_len] - gc_row, -jnp.inf))
        k_beta = k * beta
        lower = jnp.where(strict, _bdot(k_beta, k, _BNT) * decay, 0.0)
        intra = _bdot(q, k, _BNT) * decay
        inv = _unit_lower_inverse(lower)
        sol = _bdot(inv, jnp.concatenate([v * beta, k_beta * e_gc], axis=-1), _BNN)
        u = sol[:, :, :dh]
        w = sol[:, :, dh:]
        q_dec = q * e_gc
        k_dec = k * jnp.exp(g_last - gc)
        e_last = jnp.exp(g_last)
        zc = z_ref[pl.ds(r0, rows), :]
        state = state_sc[...]
        for g in range(grp):
            sl = slice(g * n_heads, (g + 1) * n_heads)
            v_new = u[sl] - _bdot(w[sl], state, _BNN)
            o = _bdot(q_dec[sl], state, _BNN) + _bdot(intra[sl], v_new, _BNN)
            state = state * e_last[sl] + _bdot(k_dec[sl], v_new, _BTN)
            on = _rms_norm(o, og)
            zz = _silu(zc[g * c_len:(g + 1) * c_len, :])
            o_ref[pl.ds(r0 + g * c_len, c_len), :] = jnp.concatenate([on[h] for h in range(n_heads)], axis=-1) * zz
        state_sc[...] = state
        return carry

    lax.fori_loop(0, n_groups, group, 0)


def _gdn(qkv, z, ab, conv_w, a_log, dt_bias, out_gain, *, batch, seq, seg=1024):
    t = batch * seq
    n_seg = seq // seg
    row_map = lambda b, s: (b * n_seg + s, 0)
    return pl.pallas_call(
        functools.partial(_gdn_kernel, n_groups=seg // (GDN_CHUNK * _GDN_GROUP)),
        grid=(batch, n_seg),
        in_specs=[
            pl.BlockSpec((seg, 3 * GDN_WIDTH), row_map),
            pl.BlockSpec((seg, GDN_WIDTH), row_map),
            pl.BlockSpec((seg, LANES), row_map),
            _const_spec((CONV_WIDTH, 3 * GDN_WIDTH)),
            _const_spec((1, LANES)),
            _const_spec((1, LANES)),
            _const_spec((1, GDN_HEAD_DIM)),
        ],
        out_specs=pl.BlockSpec((seg, GDN_WIDTH), row_map),
        out_shape=jax.ShapeDtypeStruct((t, GDN_WIDTH), F32),
        scratch_shapes=[
            pltpu.VMEM((GDN_HEADS, GDN_HEAD_DIM, GDN_HEAD_DIM), F32),
            pltpu.VMEM((CONV_TAIL + seg, 3 * GDN_WIDTH), F32),
            pltpu.VMEM((seg, 3 * GDN_WIDTH), F32),
        ],
        compiler_params=pltpu.CompilerParams(
            dimension_semantics=("arbitrary", "arbitrary"), vmem_limit_bytes=VMEM_LIMIT),
        name="gdn",
    )(qkv, z, ab, conv_w, a_log, dt_bias, out_gain)


_PAIR = LANES // MOBA_HEAD_DIM
_MOBA_STEP_HEADS = 4


def _norm_rope_pair(x, gain, cos_t, sin_t):
    d = MOBA_HEAD_DIM
    lane = lax.broadcasted_iota(jnp.int32, (1, LANES), 1)
    first = lane < d
    x2 = x * x
    s_a = jnp.sum(jnp.where(first, x2, 0.0), axis=-1, keepdims=True)
    s_b = jnp.sum(jnp.where(first, 0.0, x2), axis=-1, keepdims=True)
    ms = jnp.where(first, s_a, s_b) * (1.0 / d)
    xn = x * lax.rsqrt(ms + NORM_EPS) * gain
    half = ROPE_DIM // 2
    partner = jnp.where((lane & (d - 1)) < half,
                        pltpu.roll(xn, LANES - half, axis=1),
                        pltpu.roll(xn, half, axis=1))
    return xn * cos_t + partner * sin_t


def _moba_kernel(q_ref, k_ref, v_ref, qg_ref, kg_ref, cq_ref, sq_ref, ck_ref, sk_ref, o_ref,
                 k_sc, vt_sc, km_sc, p_sc, *, n_blocks, n_heads):
    blk, d = MOBA_BLOCK, MOBA_HEAD_DIM
    qb = pl.program_id(2)
    neg_inf = -jnp.inf
    heads = range(n_heads)

    def slab(ref_rows, pr):
        return ref_rows[:, pr * LANES:(pr + 1) * LANES]

    @pl.when(qb == 0)
    def _prep_keys():
        for j in range(n_blocks):
            rows = slice(j * blk, (j + 1) * blk)
            for pr in range(n_heads // _PAIR):
                kn = _norm_rope_pair(slab(k_ref[0, rows, :], pr), kg_ref[...], ck_ref[rows, :], sk_ref[rows, :])
                vt = slab(v_ref[0, rows, :], pr).T
                for h2 in range(_PAIR):
                    hh = pr * _PAIR + h2
                    kh = kn[:, h2 * d:(h2 + 1) * d]
                    km_sc[hh, j:j + 1, :] = jnp.mean(kh, axis=0, keepdims=True)
                    k_sc[hh, j] = kh.astype(BF16)
                    vt_sc[hh, j] = vt[h2 * d:(h2 + 1) * d, :].astype(BF16)

    blk_id = lax.broadcasted_iota(jnp.int32, (n_blocks, blk), 0)
    blk_f = blk_id.astype(F32)
    key_i = lax.broadcasted_iota(jnp.int32, (blk, blk), 0)
    qry_i = lax.broadcasted_iota(jnp.int32, (blk, blk), 1)

    q_scaled, picks, stats0 = [], [], []
    for pr in range(n_heads // _PAIR):
        qt_pair = _norm_rope_pair(slab(q_ref[0], pr), qg_ref[...], cq_ref[...], sq_ref[...]).T
        for h2 in range(_PAIR):
            hh = pr * _PAIR + h2
            qt = qt_pair[h2 * d:(h2 + 1) * d, :]
            gate = jnp.where(blk_id < qb, _dot3(km_sc[hh], qt), neg_inf)
            pk = []
            for r in range(MOBA_TOPK):
                top = jnp.max(gate, axis=0, keepdims=True)
                idx = jnp.min(jnp.where(gate == top, blk_f, float(n_blocks)), axis=0, keepdims=True)
                pk.append(jnp.where(r < qb, idx, -1.0))
                gate = jnp.where(blk_f == idx, neg_inf, gate)
            picks.append(pk)
            qs = (qt * (d ** -0.5)).astype(BF16)
            q_scaled.append(qs)
            s_t = jnp.where(key_i <= qry_i, _bdot(k_sc[hh, qb], qs), neg_inf)
            m = jnp.max(s_t, axis=0, keepdims=True)
            p_t = jnp.exp(s_t - m)
            p_sc[hh] = p_t.astype(BF16)
            stats0.append((m, jnp.sum(p_t, axis=0, keepdims=True), jnp.zeros((d, blk), F32)))

    def past_block(j, carry):
        jf = j.astype(F32)
        pending = jnp.where(j == 0, qb, j - 1)
        s_all = [_bdot(k_sc[hh, j], q_scaled[hh]) for hh in heads]
        pv_all = [_bdot(vt_sc[hh, pending], p_sc[hh]) for hh in heads]
        out = []
        for hh in heads:
            m, l, acc = carry[hh]
            pk = picks[hh]
            chosen = (pk[0] == jf) | (pk[1] == jf) | (pk[2] == jf)
            s_t = jnp.where(chosen, s_all[hh], neg_inf)
            m_new = jnp.maximum(m, jnp.max(s_t, axis=0, keepdims=True))
            alpha = jnp.exp(m - m_new)
            p_t = jnp.exp(s_t - m_new)
            p_sc[hh] = p_t.astype(BF16)
            out.append((m_new, alpha * l + jnp.sum(p_t, axis=0, keepdims=True), alpha * (acc + pv_all[hh])))
        return tuple(out)

    final = lax.fori_loop(0, qb, past_block, tuple(stats0))
    last = jnp.maximum(qb - 1, 0)
    o_t = jnp.concatenate([(acc + _bdot(vt_sc[hh, last], p_sc[hh])) / l
                           for hh, (_, l, acc) in enumerate(final)], axis=0)
    o_ref[0] = o_t.T


def _moba(mq, mk, mv, q_gain, k_gain, cos_t, sin_t, *, batch, seq):
    n_blocks = seq // MOBA_BLOCK
    n_heads = _MOBA_STEP_HEADS
    cols = n_heads * MOBA_HEAD_DIM
    shape3 = (batch, seq, MOBA_WIDTH)
    q_spec = pl.BlockSpec((1, MOBA_BLOCK, cols), lambda b, g, i: (b, i, g))
    kv_spec = pl.BlockSpec((1, seq, cols), lambda b, g, i: (b, 0, g))
    out = pl.pallas_call(
        functools.partial(_moba_kernel, n_blocks=n_blocks, n_heads=n_heads),
        grid=(batch, MOBA_HEADS // n_heads, n_blocks),
        in_specs=[
            q_spec, kv_spec, kv_spec,
            _const_spec((1, LANES)), _const_spec((1, LANES)),
            pl.BlockSpec((MOBA_BLOCK, LANES), lambda b, g, i: (i, 0)),
            pl.BlockSpec((MOBA_BLOCK, LANES), lambda b, g, i: (i, 0)),
            _const_spec((seq, LANES)), _const_spec((seq, LANES)),
        ],
        out_specs=q_spec,
        out_shape=jax.ShapeDtypeStruct(shape3, F32),
        scratch_shapes=[
            pltpu.VMEM((n_heads, n_blocks, MOBA_BLOCK, MOBA_HEAD_DIM), BF16),
            pltpu.VMEM((n_heads, n_blocks, MOBA_HEAD_DIM, MOBA_BLOCK), BF16),
            pltpu.VMEM((n_heads, n_blocks, MOBA_HEAD_DIM), F32),
            pltpu.VMEM((n_heads, MOBA_BLOCK, MOBA_BLOCK), BF16),
        ],
        compiler_params=pltpu.CompilerParams(
            dimension_semantics=("arbitrary", "arbitrary", "arbitrary"), vmem_limit_bytes=VMEM_LIMIT),
        name="moba",
    )(mq.reshape(shape3), mk.reshape(shape3), mv.reshape(shape3), q_gain, k_gain, cos_t, sin_t, cos_t, sin_t)
    return out.reshape(batch * seq, MOBA_WIDTH)


def _rope_tables(seq):
    half = ROPE_DIM // 2
    inv_freq = jnp.power(jnp.float32(ROPE_THETA), -jnp.arange(half, dtype=F32) * 2.0 / ROPE_DIM)
    ang = jnp.arange(seq).astype(F32)[:, None] * inv_freq[None, :]
    cos, sin = jnp.cos(ang), jnp.sin(ang)
    rest = MOBA_HEAD_DIM - ROPE_DIM
    cos_h = jnp.concatenate([cos, cos, jnp.ones((seq, rest), F32)], axis=-1)
    sin_h = jnp.concatenate([-sin, sin, jnp.zeros((seq, rest), F32)], axis=-1)
    return jnp.tile(cos_h, (1, _PAIR)), jnp.tile(sin_h, (1, _PAIR))


def _outproj_kernel(x_ref, a_ref, b_ref, wa_ref, wb_ref, o_ref):
    o_ref[...] = (x_ref[...] + _bdot(a_ref[...], wa_ref[...]) + _bdot(b_ref[...], wb_ref[...]))


def _outproj(x, o_a, o_b, w_a, w_b, *, tm=1024):
    t, d = x.shape
    return pl.pallas_call(
        _outproj_kernel,
        grid=(t // tm,),
        in_specs=[
            pl.BlockSpec((tm, d), lambda i: (i, 0)),
            pl.BlockSpec((tm, o_a.shape[1]), lambda i: (i, 0)),
            pl.BlockSpec((tm, o_b.shape[1]), lambda i: (i, 0)),
            _const_spec(w_a.shape),
            _const_spec(w_b.shape),
        ],
        out_specs=pl.BlockSpec((tm, d), lambda i: (i, 0)),
        out_shape=jax.ShapeDtypeStruct((t, d), F32),
        compiler_params=pltpu.CompilerParams(dimension_semantics=("parallel",), vmem_limit_bytes=VMEM_LIMIT),
        name="outproj",
    )(x, o_a, o_b, w_a, w_b)


def kernel(x, ffn1_norm, ffn1_w_gate, ffn1_w_up, ffn1_w_down, mix_norm, w_in, gdn_conv, gdn_a_log, gdn_dt_bias,
           gdn_out_norm, moba_q_norm, moba_k_norm, w_out, ffn2_norm, ffn2_w_gate, ffn2_w_up, ffn2_w_down):
    batch, seq, d_model = x.shape
    depth = w_in.shape[0]
    off_z = 3 * GDN_WIDTH
    off_ab = 4 * GDN_WIDTH
    off_moba = off_ab + 2 * GDN_HEADS
    cos_t, sin_t = _rope_tables(seq)
    xf = x.reshape(batch * seq, d_model)
    for l in range(depth):
        xf = _ffn(xf, ffn1_norm[l][None], ffn1_w_gate[l].astype(BF16), ffn1_w_up[l].astype(BF16),
                  ffn1_w_down[l].astype(BF16))
        w = w_in[l]
        w_ab = jnp.pad(w[:, off_ab:off_moba], ((0, 0), (0, LANES - 2 * GDN_HEADS)))
        w_all = jnp.concatenate([w[:, :off_ab], w_ab, w[:, off_moba:]], axis=1).astype(BF16)
        qkv, z, ab, mq, mk, mv = _inproj(xf, mix_norm[l][None], w_all)
        a_log = jnp.pad(gdn_a_log[l].astype(F32), (0, LANES - GDN_HEADS))[None]
        dt_b = jnp.pad(gdn_dt_bias[l].astype(F32), (0, LANES - GDN_HEADS))[None]
        o_gdn = _gdn(qkv, z, ab, gdn_conv[l], a_log, dt_b, gdn_out_norm[l][None], batch=batch, seq=seq)
        o_moba = _moba(mq, mk, mv, jnp.tile(moba_q_norm[l], _PAIR)[None], jnp.tile(moba_k_norm[l], _PAIR)[None],
                       cos_t, sin_t, batch=batch, seq=seq)
        wo = w_out[l].astype(BF16)
        xf = _outproj(xf, o_gdn, o_moba, wo[:GDN_WIDTH], wo[GDN_WIDTH:])
        xf = _ffn(xf, ffn2_norm[l][None], ffn2_w_gate[l].astype(BF16), ffn2_w_up[l].astype(BF16),
                  ffn2_w_down[l].astype(BF16))
    return xf.reshape(batch, seq, d_model)
```

```python
import functools
import math

import jax
import jax.numpy as jnp
from jax import lax
from jax.experimental import pallas as pl
from jax.experimental.pallas import tpu as pltpu

F32 = jnp.float32
BF16 = jnp.bfloat16

NORM_EPS = 1e-6
GDN_HEADS = 4
GDN_HEAD_DIM = 128
GDN_WIDTH = GDN_HEADS * GDN_HEAD_DIM
GDN_CHUNK = 64
CONV_WIDTH = 4
MOBA_HEADS = 8
MOBA_HEAD_DIM = 64
MOBA_WIDTH = MOBA_HEADS * MOBA_HEAD_DIM
MOBA_BLOCK = 256
MOBA_TOPK = 3
ROPE_THETA = 500000.0
ROPE_DIM = MOBA_HEAD_DIM // 4
LANES = 128
CONV_TAIL = 8
VMEM_LIMIT = 56 * 1024 * 1024

_NT = (((1,), (1,)), ((), ()))
_TN = (((0,), (0,)), ((), ()))


def _bdot(a, b, dims=None):
    a = a.astype(BF16)
    b = b.astype(BF16)
    if dims is None:
        return jnp.dot(a, b, preferred_element_type=F32)
    return lax.dot_general(a, b, dims, preferred_element_type=F32)


def _split(a):
    hi = a.astype(BF16)
    lo = (a - hi.astype(F32)).astype(BF16)
    return hi, lo


def _dot3(a, b, dims=None):
    a_hi, a_lo = _split(a)
    b_hi, b_lo = _split(b)
    return _bdot(a_hi, b_hi, dims) + (_bdot(a_hi, b_lo, dims) + _bdot(a_lo, b_hi, dims))


def _silu(x):
    return x * jax.nn.sigmoid(x)


def _softplus(x):
    return jnp.maximum(x, 0.0) + jnp.log1p(jnp.exp(-jnp.abs(x)))


def _rms_norm(x, g):
    return x * lax.rsqrt(jnp.mean(x * x, axis=-1, keepdims=True) + NORM_EPS) * g


def _ffn_kernel(*refs, n_split, n_mix):
    x_ref = refs[0]
    mix_refs = refs[1:1 + 2 * n_mix]
    g_ref, wg_ref, wu_ref, wd_ref, o_ref = refs[1 + 2 * n_mix:]
    x = x_ref[...]
    for i in range(n_mix):
        x = x + _bdot(mix_refs[i][...], mix_refs[n_mix + i][...])
    h = _rms_norm(x, g_ref[...]).astype(BF16)
    tf = wg_ref.shape[1] // n_split
    acc = None
    for j in range(n_split):
        gate = jnp.dot(h, wg_ref[:, j * tf:(j + 1) * tf], preferred_element_type=F32)
        up = jnp.dot(h, wu_ref[:, j * tf:(j + 1) * tf], preferred_element_type=F32)
        act = (_silu(gate) * up).astype(BF16)
        y = jnp.dot(act, wd_ref[j * tf:(j + 1) * tf, :], preferred_element_type=F32)
        acc = y if acc is None else acc + y
    o_ref[...] = x + 0.5 * acc


def _const_spec(shape):
    return pl.BlockSpec(shape, lambda *_: (0,) * len(shape), pipeline_mode=pl.Buffered(1))


def _ffn(x, norm_g, wg, wu, wd, mix=(), mix_w=(), *, tm=512, n_split=2):
    t, d = x.shape
    ff = wg.shape[1]
    row_spec = lambda width: pl.BlockSpec((tm, width), lambda i: (i, 0))
    return pl.pallas_call(
        functools.partial(_ffn_kernel, n_split=n_split, n_mix=len(mix)),
        grid=(t // tm,),
        in_specs=[row_spec(d)] + [row_spec(m.shape[1]) for m in mix] + [_const_spec(w.shape) for w in mix_w] + [
            _const_spec((1, d)),
            _const_spec((d, ff)),
            _const_spec((d, ff)),
            _const_spec((ff, d)),
        ],
        out_specs=row_spec(d),
        out_shape=jax.ShapeDtypeStruct((t, d), F32),
        compiler_params=pltpu.CompilerParams(dimension_semantics=("parallel",), vmem_limit_bytes=VMEM_LIMIT),
        name="ffn",
    )(x, *mix, *mix_w, norm_g, wg, wu, wd)


_IN_SPLITS = (3 * GDN_WIDTH, GDN_WIDTH, LANES, MOBA_WIDTH, MOBA_WIDTH, MOBA_WIDTH)


def _inproj_kernel(x_ref, g_ref, w_ref, *out_refs):
    h = _rms_norm(x_ref[...], g_ref[...]).astype(BF16)
    p = jnp.dot(h, w_ref[...], preferred_element_type=F32)
    off = 0
    for ref, width in zip(out_refs, _IN_SPLITS):
        ref[...] = p[:, off:off + width]
        off += width


def _inproj(x, norm_g, w_all, *, tm=512):
    t, d = x.shape
    n = w_all.shape[1]
    return pl.pallas_call(
        _inproj_kernel,
        grid=(t // tm,),
        in_specs=[pl.BlockSpec((tm, d), lambda i: (i, 0)), _const_spec((1, d)), _const_spec((d, n))],
        out_specs=[pl.BlockSpec((tm, w), lambda i: (i, 0)) for w in _IN_SPLITS],
        out_shape=[jax.ShapeDtypeStruct((t, w), F32) for w in _IN_SPLITS],
        compiler_params=pltpu.CompilerParams(dimension_semantics=("parallel",), vmem_limit_bytes=VMEM_LIMIT),
        name="inproj",
    )(x, norm_g, w_all)


_GDN_GROUP = 4
_BNT = (((2,), (2,)), ((0,), (0,)))
_BNN = (((2,), (1,)), ((0,), (0,)))
_BTN = (((1,), (1,)), ((0,), (0,)))


def _chunk_cumsum(x, c_len):
    pos = lax.broadcasted_iota(jnp.int32, x.shape, 0) & (c_len - 1)
    s = 1
    while s < c_len:
        x = x + jnp.where(pos >= s, pltpu.roll(x, s, axis=0), 0.0)
        s *= 2
    return x


def _unit_lower_inverse(lower, base=16):
    c = lower.shape[-1]
    ri = lax.broadcasted_iota(jnp.int32, (1, c, c), 1)
    ci = lax.broadcasted_iota(jnp.int32, (1, c, c), 2)
    same = lambda size: (ri // size) == (ci // size)
    pw = jnp.where(same(base), -lower, 0.0)
    inv = jnp.where(ri == ci, 1.0, 0.0) + pw
    sq = 1
    while 2 * sq < base:
        pw = _bdot(pw, pw, _BNN)
        inv = inv + _bdot(inv, pw, _BNN)
        sq *= 2
    size = base
    while size < c:
        off = jnp.where(same(size), 0.0, jnp.where(same(2 * size), lower, 0.0))
        inv = inv - _bdot(_bdot(inv, off, _BNN), inv, _BNN)
        size *= 2
    return inv


def _gdn_kernel(qkv_ref, z_ref, ab_ref, cw_ref, alog_ref, dtb_ref, og_ref, o_ref, state_sc, xpad_sc, xc_sc, *, n_groups):
    c_len, dh, n_heads, width, grp = GDN_CHUNK, GDN_HEAD_DIM, GDN_HEADS, GDN_WIDTH, _GDN_GROUP
    rows = grp * c_len
    nb = grp * n_heads

    seg = qkv_ref.shape[0]

    @pl.when(pl.program_id(1) == 0)
    def _reset():
        state_sc[...] = jnp.zeros_like(state_sc)
        xpad_sc[0:CONV_TAIL, :] = jnp.zeros((CONV_TAIL, xpad_sc.shape[1]), F32)

    @pl.when(pl.program_id(1) != 0)
    def _carry_history():
        xpad_sc[0:CONV_TAIL, :] = xpad_sc[seg:seg + CONV_TAIL, :]

    xpad_sc[CONV_TAIL:CONV_TAIL + seg, :] = qkv_ref[...]
    cw = cw_ref[...]
    for t in range(seg // rows):
        win = xpad_sc[t * rows:t * rows + CONV_TAIL + rows, :]
        conv = win * cw[CONV_WIDTH - 1:CONV_WIDTH]
        for back in range(1, CONV_WIDTH):
            conv = conv + pltpu.roll(win, back, axis=0) * cw[CONV_WIDTH - 1 - back:CONV_WIDTH - back]
        xc_sc[t * rows:(t + 1) * rows, :] = _silu(conv[CONV_TAIL:])

    ri = lax.broadcasted_iota(jnp.int32, (1, c_len, c_len), 1)
    ci = lax.broadcasted_iota(jnp.int32, (1, c_len, c_len), 2)
    causal = ri >= ci
    strict = ri > ci
    og = og_ref[...]

    def tiles(x, off):
        return jnp.stack([x[g * c_len:(g + 1) * c_len, off + h * dh:off + (h + 1) * dh]
                          for g in range(grp) for h in range(n_heads)])

    def lane_tiles(x, lane0):
        return jnp.stack([jnp.broadcast_to(x[g * c_len:(g + 1) * c_len, lane0 + h:lane0 + h + 1], (c_len, dh))
                          for g in range(grp) for h in range(n_heads)])

    def group(gi, carry):
        r0 = pl.multiple_of(gi * rows, rows)
        xc = xc_sc[pl.ds(r0, rows), :]
        ab = ab_ref[pl.ds(r0, rows), :]
        g_tok = -jnp.exp(alog_ref[...]) * _softplus(ab + dtb_ref[...])
        gc_tok = _chunk_cumsum(g_tok, c_len)
        gc = lane_tiles(gc_tok, 0)
        gc_t = gc_tok.T
        gc_row = jnp.stack([jnp.broadcast_to(gc_t[h:h + 1, g * c_len:(g + 1) * c_len], (c_len, c_len))
                            for g in range(grp) for h in range(n_heads)])
        beta = lane_tiles(jax.nn.sigmoid(ab), n_heads)
        q = tiles(xc, 0)
        k = tiles(xc, width)
        v = tiles(xc, 2 * width)
        q = q * lax.rsqrt(jnp.sum(q * q, axis=-1, keepdims=True) + NORM_EPS) * (dh ** -0.5)
        k = k * lax.rsqrt(jnp.sum(k * k, axis=-1, keepdims=True) + NORM_EPS)
        g_last = gc[:, c_len - 1:c_len, :]
        e_gc = jnp.exp(gc)
        decay = jnp.exp(jnp.where(causal, gc[:, :, :c_len] - gc_row, -jnp.inf))
        k_beta = k * beta
        lower = jnp.where(strict, _bdot(k_beta, k, _BNT) * decay, 0.0)
        intra = _bdot(q, k, _BNT) * decay
        inv = _unit_lower_inverse(lower)
        sol = _bdot(inv, jnp.concatenate([v * beta, k_beta * e_gc], axis=-1), _BNN)
        u = sol[:, :, :dh]
        w = sol[:, :, dh:]
        q_dec = q * e_gc
        k_dec = k * jnp.exp(g_last - gc)
        e_last = jnp.exp(g_last)
        zc = z_ref[pl.ds(r0, rows), :]
        state = state_sc[...]
        for g in range(grp):
            sl = slice(g * n_heads, (g + 1) * n_heads)
            v_new = u[sl] - _bdot(w[sl], state, _BNN)
            o = _bdot(q_dec[sl], state, _BNN) + _bdot(intra[sl], v_new, _BNN)
            state = state * e_last[sl] + _bdot(k_dec[sl], v_new, _BTN)
            on = _rms_norm(o, og)
            zz = _silu(zc[g * c_len:(g + 1) * c_len, :])
            o_ref[pl.ds(r0 + g * c_len, c_len), :] = jnp.concatenate([on[h] for h in range(n_heads)], axis=-1) * zz
        state_sc[...] = state
        return carry

    lax.fori_loop(0, n_groups, group, 0)


def _gdn(qkv, z, ab, conv_w, a_log, dt_bias, out_gain, *, batch, seq, seg=1024):
    t = batch * seq
    n_seg = seq // seg
    row_map = lambda b, s: (b * n_seg + s, 0)
    return pl.pallas_call(
        functools.partial(_gdn_kernel, n_groups=seg // (GDN_CHUNK * _GDN_GROUP)),
        grid=(batch, n_seg),
        in_specs=[
            pl.BlockSpec((seg, 3 * GDN_WIDTH), row_map),
            pl.BlockSpec((seg, GDN_WIDTH), row_map),
            pl.BlockSpec((seg, LANES), row_map),
            _const_spec((CONV_WIDTH, 3 * GDN_WIDTH)),
            _const_spec((1, LANES)),
            _const_spec((1, LANES)),
            _const_spec((1, GDN_HEAD_DIM)),
        ],
        out_specs=pl.BlockSpec((seg, GDN_WIDTH), row_map),
        out_shape=jax.ShapeDtypeStruct((t, GDN_WIDTH), F32),
        scratch_shapes=[
            pltpu.VMEM((GDN_HEADS, GDN_HEAD_DIM, GDN_HEAD_DIM), F32),
            pltpu.VMEM((CONV_TAIL + seg, 3 * GDN_WIDTH), F32),
            pltpu.VMEM((seg, 3 * GDN_WIDTH), F32),
        ],
        compiler_params=pltpu.CompilerParams(
            dimension_semantics=("arbitrary", "arbitrary"), vmem_limit_bytes=VMEM_LIMIT),
        name="gdn",
    )(qkv, z, ab, conv_w, a_log, dt_bias, out_gain)


_PAIR = LANES // MOBA_HEAD_DIM
_MOBA_STEP_HEADS = 4
_MOBA_TRIP_BLOCKS = 2
_VT_ROWS = MOBA_HEAD_DIM + 16
_LOG2E = math.log2(math.e)


def _norm_rope_pair(x, gain, cos_t, sin_t):
    d = MOBA_HEAD_DIM
    lane = lax.broadcasted_iota(jnp.int32, (1, LANES), 1)
    ri = lax.broadcasted_iota(jnp.int32, (LANES, LANES), 0)
    ci = lax.broadcasted_iota(jnp.int32, (LANES, LANES), 1)
    same_head = jnp.where((ri // d) == (ci // d), 1.0, 0.0).astype(BF16)
    hi, lo = _split(x * x)
    ms = (_bdot(hi, same_head) + _bdot(lo, same_head)) * (1.0 / d)
    xn = x * lax.rsqrt(ms + NORM_EPS) * gain
    half = ROPE_DIM // 2
    partner = jnp.where((lane & (d - 1)) < half,
                        pltpu.roll(xn, LANES - half, axis=1),
                        pltpu.roll(xn, half, axis=1))
    return xn * cos_t + partner * sin_t


def _moba_kernel(q_ref, k_ref, v_ref, qg_ref, kg_ref, cq_ref, sq_ref, ck_ref, sk_ref, o_ref,
                 k_sc, vt_sc, km_sc, p_sc, s_sc, *, n_blocks, n_heads):
    blk, d, nt = MOBA_BLOCK, MOBA_HEAD_DIM, _MOBA_TRIP_BLOCKS
    qb = pl.program_id(2)
    neg_inf = -jnp.inf
    heads = range(n_heads)

    def slab(ref_rows, pr):
        return ref_rows[:, pr * LANES:(pr + 1) * LANES]

    @pl.when(qb == 0)
    def _prep_keys():
        extra = lax.broadcasted_iota(jnp.int32, (_VT_ROWS - d, blk), 0)
        ones_row = jnp.where(extra == 0, 1.0, 0.0).astype(BF16)
        for j in range(n_blocks):
            rows = slice(j * blk, (j + 1) * blk)
            for pr in range(n_heads // _PAIR):
                kn = _norm_rope_pair(slab(k_ref[0, rows, :], pr), kg_ref[...], ck_ref[rows, :], sk_ref[rows, :])
                km_sc[pr, j:j + 1, :] = jnp.mean(kn, axis=0, keepdims=True)
                k_sc[pr, j] = kn.astype(BF16)
                vt = slab(v_ref[0, rows, :], pr).T
                for h2 in range(_PAIR):
                    hh = pr * _PAIR + h2
                    vt_sc[hh, j, 0:d, :] = vt[h2 * d:(h2 + 1) * d, :].astype(BF16)
                    vt_sc[hh, j, d:, :] = ones_row

    blk_id = lax.broadcasted_iota(jnp.int32, (n_blocks, blk), 0)
    blk_f = blk_id.astype(F32)
    key_i = lax.broadcasted_iota(jnp.int32, (blk, blk), 0)
    qry_i = lax.broadcasted_iota(jnp.int32, (blk, blk), 1)
    dim_i = lax.broadcasted_iota(jnp.int32, (LANES, blk), 0)

    q_scaled, picks, stats0 = [], [], []
    for pr in range(n_heads // _PAIR):
        qt_pair = _norm_rope_pair(slab(q_ref[0], pr), qg_ref[...], cq_ref[...], sq_ref[...]).T
        for h2 in range(_PAIR):
            hh = pr * _PAIR + h2
            qt = jnp.where((dim_i // d) == h2, qt_pair, 0.0)
            gate = jnp.where(blk_id < qb, _dot3(km_sc[pr], qt), neg_inf)
            pk = []
            for r in range(MOBA_TOPK):
                top = jnp.max(gate, axis=0, keepdims=True)
                idx = jnp.min(jnp.where(gate == top, blk_f, float(n_blocks)), axis=0, keepdims=True)
                pk.append(jnp.where(r < qb, idx, -1.0))
                gate = jnp.where(blk_f == idx, neg_inf, gate)
            picks.append(pk)
            qs = (qt * (d ** -0.5 * _LOG2E)).astype(BF16)
            q_scaled.append(qs)
            s_t = jnp.where(key_i <= qry_i, _bdot(k_sc[hh // _PAIR, qb], qs), neg_inf)
            m = jnp.max(s_t, axis=0, keepdims=True)
            p_sc[hh, 0] = jnp.exp2(s_t - m).astype(BF16)
            for a in range(1, nt):
                p_sc[hh, a] = jnp.zeros((blk, blk), BF16)
            stats0.append((m, jnp.zeros((_VT_ROWS, blk), F32)))
            for a in range(nt):
                s_sc[hh, a] = _bdot(k_sc[hh // _PAIR, a], qs)

    def pending_block(t, a):
        return jnp.where(t == 0, qb, jnp.minimum((t - 1) * nt + a, n_blocks - 1))

    def pending_pv(t, hh):
        pv = None
        for a in range(nt):
            term = _bdot(vt_sc[hh, pending_block(t, a)], p_sc[hh, a])
            pv = term if pv is None else pv + term
        return pv

    def trip(t, carry):
        pv_all = [pending_pv(t, hh) for hh in heads]
        s_next = [[_bdot(k_sc[hh // _PAIR, jnp.minimum((t + 1) * nt + a, n_blocks - 1)], q_scaled[hh]) for a in range(nt)]
                  for hh in heads]
        out = []
        for hh in heads:
            m, acc = carry[hh]
            pk = picks[hh]
            s_t, chosen, m_new = [], [], m
            for a in range(nt):
                jf = (t * nt + a).astype(F32)
                chosen.append((pk[0] == jf) | (pk[1] == jf) | (pk[2] == jf))
                s_t.append(s_sc[hh, a])
                m_new = jnp.where(chosen[a], jnp.maximum(m_new, jnp.max(s_t[a], axis=0, keepdims=True)), m_new)
            for a in range(nt):
                shift = jnp.where(chosen[a], m_new, jnp.inf)
                p_sc[hh, a] = jnp.exp2(s_t[a] - shift).astype(BF16)
            out.append((m_new, jnp.exp2(m - m_new) * (acc + pv_all[hh])))
        for hh in heads:
            for a in range(nt):
                s_sc[hh, a] = s_next[hh][a]
        return tuple(out)

    n_trips = (qb + nt - 1) // nt
    final = lax.fori_loop(0, n_trips, trip, tuple(stats0))
    outs = []
    for hh, (_, acc) in enumerate(final):
        acc = acc + pending_pv(n_trips, hh)
        outs.append(acc[:d] / acc[d:d + 1])
    o_ref[0] = jnp.concatenate(outs, axis=0).T


def _moba(mq, mk, mv, q_gain, k_gain, cos_t, sin_t, *, batch, seq):
    n_blocks = seq // MOBA_BLOCK
    n_heads = _MOBA_STEP_HEADS
    cols = n_heads * MOBA_HEAD_DIM
    shape3 = (batch, seq, MOBA_WIDTH)
    q_spec = pl.BlockSpec((1, MOBA_BLOCK, cols), lambda b, g, i: (b, i, g))
    kv_spec = pl.BlockSpec((1, seq, cols), lambda b, g, i: (b, 0, g))
    out = pl.pallas_call(
        functools.partial(_moba_kernel, n_blocks=n_blocks, n_heads=n_heads),
        grid=(batch, MOBA_HEADS // n_heads, n_blocks),
        in_specs=[
            q_spec, kv_spec, kv_spec,
            _const_spec((1, LANES)), _const_spec((1, LANES)),
            pl.BlockSpec((MOBA_BLOCK, LANES), lambda b, g, i: (i, 0)),
            pl.BlockSpec((MOBA_BLOCK, LANES), lambda b, g, i: (i, 0)),
            _const_spec((seq, LANES)), _const_spec((seq, LANES)),
        ],
        out_specs=q_spec,
        out_shape=jax.ShapeDtypeStruct(shape3, F32),
        scratch_shapes=[
            pltpu.VMEM((n_heads // _PAIR, n_blocks, MOBA_BLOCK, LANES), BF16),
            pltpu.VMEM((n_heads, n_blocks, _VT_ROWS, MOBA_BLOCK), BF16),
            pltpu.VMEM((n_heads // _PAIR, n_blocks, LANES), F32),
            pltpu.VMEM((n_heads, _MOBA_TRIP_BLOCKS, MOBA_BLOCK, MOBA_BLOCK), BF16),
            pltpu.VMEM((n_heads, _MOBA_TRIP_BLOCKS, MOBA_BLOCK, MOBA_BLOCK), F32),
        ],
        compiler_params=pltpu.CompilerParams(
            dimension_semantics=("arbitrary", "arbitrary", "arbitrary"), vmem_limit_bytes=VMEM_LIMIT),
        name="moba",
    )(mq.reshape(shape3), mk.reshape(shape3), mv.reshape(shape3), q_gain, k_gain, cos_t, sin_t, cos_t, sin_t)
    return out.reshape(batch * seq, MOBA_WIDTH)


def _rope_tables(seq):
    half = ROPE_DIM // 2
    inv_freq = jnp.power(jnp.float32(ROPE_THETA), -jnp.arange(half, dtype=F32) * 2.0 / ROPE_DIM)
    ang = jnp.arange(seq).astype(F32)[:, None] * inv_freq[None, :]
    cos, sin = jnp.cos(ang), jnp.sin(ang)
    rest = MOBA_HEAD_DIM - ROPE_DIM
    cos_h = jnp.concatenate([cos, cos, jnp.ones((seq, rest), F32)], axis=-1)
    sin_h = jnp.concatenate([-sin, sin, jnp.zeros((seq, rest), F32)], axis=-1)
    return jnp.tile(cos_h, (1, _PAIR)), jnp.tile(sin_h, (1, _PAIR))


def kernel(x, ffn1_norm, ffn1_w_gate, ffn1_w_up, ffn1_w_down, mix_norm, w_in, gdn_conv, gdn_a_log, gdn_dt_bias,
           gdn_out_norm, moba_q_norm, moba_k_norm, w_out, ffn2_norm, ffn2_w_gate, ffn2_w_up, ffn2_w_down):
    batch, seq, d_model = x.shape
    depth = w_in.shape[0]
    off_z = 3 * GDN_WIDTH
    off_ab = 4 * GDN_WIDTH
    off_moba = off_ab + 2 * GDN_HEADS
    cos_t, sin_t = _rope_tables(seq)
    xf = x.reshape(batch * seq, d_model)
    for l in range(depth):
        xf = _ffn(xf, ffn1_norm[l][None], ffn1_w_gate[l].astype(BF16), ffn1_w_up[l].astype(BF16),
                  ffn1_w_down[l].astype(BF16))
        w = w_in[l]
        w_ab = jnp.pad(w[:, off_ab:off_moba], ((0, 0), (0, LANES - 2 * GDN_HEADS)))
        w_all = jnp.concatenate([w[:, :off_ab], w_ab, w[:, off_moba:]], axis=1).astype(BF16)
        qkv, z, ab, mq, mk, mv = _inproj(xf, mix_norm[l][None], w_all)
        a_log = jnp.pad(gdn_a_log[l].astype(F32), (0, LANES - GDN_HEADS))[None]
        dt_b = jnp.pad(gdn_dt_bias[l].astype(F32), (0, LANES - GDN_HEADS))[None]
        o_gdn = _gdn(qkv, z, ab, gdn_conv[l], a_log, dt_b, gdn_out_norm[l][None], batch=batch, seq=seq)
        o_moba = _moba(mq, mk, mv, jnp.tile(moba_q_norm[l], _PAIR)[None], jnp.tile(moba_k_norm[l], _PAIR)[None],
                       cos_t, sin_t, batch=batch, seq=seq)
        wo = w_out[l].astype(BF16)
        xf = _ffn(xf, ffn2_norm[l][None], ffn2_w_gate[l].astype(BF16), ffn2_w_up[l].astype(BF16),
                  ffn2_w_down[l].astype(BF16), mix=(o_gdn, o_moba), mix_w=(wo[:GDN_WIDTH], wo[GDN_WIDTH:]))
    return xf.reshape(batch, seq, d_model)
```

```python
import functools
import math

import jax
import jax.numpy as jnp
from jax import lax
from jax.experimental import pallas as pl
from jax.experimental.pallas import tpu as pltpu

F32 = jnp.float32
BF16 = jnp.bfloat16

NORM_EPS = 1e-6
GDN_HEADS = 4
GDN_HEAD_DIM = 128
GDN_WIDTH = GDN_HEADS * GDN_HEAD_DIM
GDN_CHUNK = 64
CONV_WIDTH = 4
MOBA_HEADS = 8
MOBA_HEAD_DIM = 64
MOBA_WIDTH = MOBA_HEADS * MOBA_HEAD_DIM
MOBA_BLOCK = 256
MOBA_TOPK = 3
ROPE_THETA = 500000.0
ROPE_DIM = MOBA_HEAD_DIM // 4
LANES = 128
MXU_TILE = 256
CONV_TAIL = 8
VMEM_LIMIT = 56 * 1024 * 1024

_NT = (((1,), (1,)), ((), ()))
_TN = (((0,), (0,)), ((), ()))


def _bdot(a, b, dims=None):
    a = a.astype(BF16)
    b = b.astype(BF16)
    if dims is None:
        return jnp.dot(a, b, preferred_element_type=F32)
    return lax.dot_general(a, b, dims, preferred_element_type=F32)


def _split(a):
    hi = a.astype(BF16)
    lo = (a - hi.astype(F32)).astype(BF16)
    return hi, lo


def _dot3(a, b, dims=None):
    a_hi, a_lo = _split(a)
    b_hi, b_lo = _split(b)
    return _bdot(a_hi, b_hi, dims) + (_bdot(a_hi, b_lo, dims) + _bdot(a_lo, b_hi, dims))


def _silu(x):
    return x * jax.nn.sigmoid(x)


def _softplus(x):
    return jnp.maximum(x, 0.0) + jnp.log1p(jnp.exp(-jnp.abs(x)))


def _rms_norm(x, g):
    return x * lax.rsqrt(jnp.mean(x * x, axis=-1, keepdims=True) + NORM_EPS) * g


def _ffn_kernel(*refs, n_split, n_mix):
    x_ref = refs[0]
    mix_refs = refs[1:1 + 2 * n_mix]
    g_ref, wg_ref, wu_ref, wd_ref, o_ref = refs[1 + 2 * n_mix:]
    x = x_ref[...]
    for i in range(n_mix):
        x = x + _bdot(mix_refs[i][...], mix_refs[n_mix + i][...])
    h = _rms_norm(x, g_ref[...]).astype(BF16)
    n_tiles = wg_ref.shape[1] // MXU_TILE
    cuts = [MXU_TILE * ((n_tiles * j + n_split - 1) // n_split) for j in range(n_split)] + [wg_ref.shape[1]]
    acc = None
    for lo, hi in zip(cuts[:-1], cuts[1:]):
        gate = jnp.dot(h, wg_ref[:, lo:hi], preferred_element_type=F32)
        up = jnp.dot(h, wu_ref[:, lo:hi], preferred_element_type=F32)
        act = (_silu(gate) * up).astype(BF16)
        y = jnp.dot(act, wd_ref[lo:hi, :], preferred_element_type=F32)
        acc = y if acc is None else acc + y
    o_ref[...] = x + 0.5 * acc


def _const_spec(shape):
    return pl.BlockSpec(shape, lambda *_: (0,) * len(shape), pipeline_mode=pl.Buffered(1))


def _ffn(x, norm_g, wg, wu, wd, mix=(), mix_w=(), *, tm=512, n_split=2):
    t, d = x.shape
    ff = wg.shape[1]
    row_spec = lambda width: pl.BlockSpec((tm, width), lambda i: (i, 0))
    return pl.pallas_call(
        functools.partial(_ffn_kernel, n_split=n_split, n_mix=len(mix)),
        grid=(t // tm,),
        in_specs=[row_spec(d)] + [row_spec(m.shape[1]) for m in mix] + [_const_spec(w.shape) for w in mix_w] + [
            _const_spec((1, d)),
            _const_spec((d, ff)),
            _const_spec((d, ff)),
            _const_spec((ff, d)),
        ],
        out_specs=row_spec(d),
        out_shape=jax.ShapeDtypeStruct((t, d), F32),
        compiler_params=pltpu.CompilerParams(dimension_semantics=("parallel",), vmem_limit_bytes=VMEM_LIMIT),
        name="ffn",
    )(x, *mix, *mix_w, norm_g, wg, wu, wd)


_IN_SPLITS = (3 * GDN_WIDTH, GDN_WIDTH, LANES, MOBA_WIDTH, MOBA_WIDTH, MOBA_WIDTH)
_IN_DTYPES = (F32, F32, F32, F32, F32, BF16)


def _inproj_kernel(x_ref, g_ref, w_ref, *out_refs):
    h = _rms_norm(x_ref[...], g_ref[...]).astype(BF16)
    p = jnp.dot(h, w_ref[...], preferred_element_type=F32)
    off = 0
    for ref, width in zip(out_refs, _IN_SPLITS):
        ref[...] = p[:, off:off + width].astype(ref.dtype)
        off += width


def _inproj(x, norm_g, w_all, *, tm=512):
    t, d = x.shape
    n = w_all.shape[1]
    return pl.pallas_call(
        _inproj_kernel,
        grid=(t // tm,),
        in_specs=[pl.BlockSpec((tm, d), lambda i: (i, 0)), _const_spec((1, d)), _const_spec((d, n))],
        out_specs=[pl.BlockSpec((tm, w), lambda i: (i, 0)) for w in _IN_SPLITS],
        out_shape=[jax.ShapeDtypeStruct((t, w), dt) for w, dt in zip(_IN_SPLITS, _IN_DTYPES)],
        compiler_params=pltpu.CompilerParams(dimension_semantics=("parallel",), vmem_limit_bytes=VMEM_LIMIT),
        name="inproj",
    )(x, norm_g, w_all)


_GDN_GROUP = 4
_BNT = (((2,), (2,)), ((0,), (0,)))
_BNN = (((2,), (1,)), ((0,), (0,)))
_BTN = (((1,), (1,)), ((0,), (0,)))


def _chunk_cumsum(x, c_len):
    pos = lax.broadcasted_iota(jnp.int32, x.shape, 0) & (c_len - 1)
    s = 1
    while s < c_len:
        x = x + jnp.where(pos >= s, pltpu.roll(x, s, axis=0), 0.0)
        s *= 2
    return x


def _unit_lower_inverse(lower, base=16):
    c = lower.shape[-1]
    ri = lax.broadcasted_iota(jnp.int32, (1, c, c), 1)
    ci = lax.broadcasted_iota(jnp.int32, (1, c, c), 2)
    same = lambda size: (ri // size) == (ci // size)
    pw = jnp.where(same(base), -lower, 0.0)
    inv = jnp.where(ri == ci, 1.0, 0.0) + pw
    sq = 1
    while 2 * sq < base:
        pw = _bdot(pw, pw, _BNN)
        inv = inv + _bdot(inv, pw, _BNN)
        sq *= 2
    size = base
    while size < c:
        off = jnp.where(same(size), 0.0, jnp.where(same(2 * size), lower, 0.0))
        inv = inv - _bdot(_bdot(inv, off, _BNN), inv, _BNN)
        size *= 2
    return inv


def _gdn_kernel(qkv_ref, z_ref, ab_ref, cw_ref, alog_ref, dtb_ref, og_ref, o_ref, state_sc, xpad_sc, xc_sc, *, n_groups):
    c_len, dh, n_heads, width, grp = GDN_CHUNK, GDN_HEAD_DIM, GDN_HEADS, GDN_WIDTH, _GDN_GROUP
    rows = grp * c_len
    nb = grp * n_heads

    seg = qkv_ref.shape[0]

    @pl.when(pl.program_id(1) == 0)
    def _reset():
        state_sc[...] = jnp.zeros_like(state_sc)
        xpad_sc[0:CONV_TAIL, :] = jnp.zeros((CONV_TAIL, xpad_sc.shape[1]), F32)

    @pl.when(pl.program_id(1) != 0)
    def _carry_history():
        xpad_sc[0:CONV_TAIL, :] = xpad_sc[seg:seg + CONV_TAIL, :]

    xpad_sc[CONV_TAIL:CONV_TAIL + seg, :] = qkv_ref[...]
    cw = cw_ref[...]
    for t in range(seg // rows):
        win = xpad_sc[t * rows:t * rows + CONV_TAIL + rows, :]
        conv = win * cw[CONV_WIDTH - 1:CONV_WIDTH]
        for back in range(1, CONV_WIDTH):
            conv = conv + pltpu.roll(win, back, axis=0) * cw[CONV_WIDTH - 1 - back:CONV_WIDTH - back]
        xc_sc[t * rows:(t + 1) * rows, :] = _silu(conv[CONV_TAIL:])

    ri = lax.broadcasted_iota(jnp.int32, (1, c_len, c_len), 1)
    ci = lax.broadcasted_iota(jnp.int32, (1, c_len, c_len), 2)
    causal = ri >= ci
    strict = ri > ci
    og = og_ref[...]

    def tiles(x, off):
        return jnp.stack([x[g * c_len:(g + 1) * c_len, off + h * dh:off + (h + 1) * dh]
                          for g in range(grp) for h in range(n_heads)])

    def lane_tiles(x, lane0):
        return jnp.stack([jnp.broadcast_to(x[g * c_len:(g + 1) * c_len, lane0 + h:lane0 + h + 1], (c_len, dh))
                          for g in range(grp) for h in range(n_heads)])

    def group(gi, carry):
        r0 = pl.multiple_of(gi * rows, rows)
        xc = xc_sc[pl.ds(r0, rows), :]
        ab = ab_ref[pl.ds(r0, rows), :]
        g_tok = -jnp.exp(alog_ref[...]) * _softplus(ab + dtb_ref[...])
        gc_tok = _chunk_cumsum(g_tok, c_len)
        gc = lane_tiles(gc_tok, 0)
        gc_t = gc_tok.T
        gc_row = jnp.stack([jnp.broadcast_to(gc_t[h:h + 1, g * c_len:(g + 1) * c_len], (c_len, c_len))
                            for g in range(grp) for h in range(n_heads)])
        beta = lane_tiles(jax.nn.sigmoid(ab), n_heads)
        q = tiles(xc, 0)
        k = tiles(xc, width)
        v = tiles(xc, 2 * width)
        q = q * lax.rsqrt(jnp.sum(q * q, axis=-1, keepdims=True) + NORM_EPS) * (dh ** -0.5)
        k = k * lax.rsqrt(jnp.sum(k * k, axis=-1, keepdims=True) + NORM_EPS)
        g_last = gc[:, c_len - 1:c_len, :]
        e_gc = jnp.exp(gc)
        decay = jnp.exp(jnp.where(causal, gc[:, :, :c_len] - gc_row, -jnp.inf))
        k_beta = k * beta
        lower = jnp.where(strict, _bdot(k_beta, k, _BNT) * decay, 0.0)
        intra = _bdot(q, k, _BNT) * decay
        inv = _unit_lower_inverse(lower)
        sol = _bdot(inv, jnp.concatenate([v * beta, k_beta * e_gc], axis=-1), _BNN)
        u = sol[:, :, :dh]
        w = sol[:, :, dh:]
        q_dec = q * e_gc
        k_dec = k * jnp.exp(g_last - gc)
        e_last = jnp.exp(g_last)
        zc = z_ref[pl.ds(r0, rows), :]
        state = state_sc[...]
        for g in range(grp):
            sl = slice(g * n_heads, (g + 1) * n_heads)
            v_new = u[sl] - _bdot(w[sl], state, _BNN)
            o = _bdot(q_dec[sl], state, _BNN) + _bdot(intra[sl], v_new, _BNN)
            state = state * e_last[sl] + _bdot(k_dec[sl], v_new, _BTN)
            on = _rms_norm(o, og)
            zz = _silu(zc[g * c_len:(g + 1) * c_len, :])
            o_ref[pl.ds(r0 + g * c_len, c_len), :] = jnp.concatenate([on[h] for h in range(n_heads)], axis=-1) * zz
        state_sc[...] = state
        return carry

    lax.fori_loop(0, n_groups, group, 0)


def _gdn(qkv, z, ab, conv_w, a_log, dt_bias, out_gain, *, batch, seq, seg=1024):
    t = batch * seq
    n_seg = seq // seg
    row_map = lambda b, s: (b * n_seg + s, 0)
    return pl.pallas_call(
        functools.partial(_gdn_kernel, n_groups=seg // (GDN_CHUNK * _GDN_GROUP)),
        grid=(batch, n_seg),
        in_specs=[
            pl.BlockSpec((seg, 3 * GDN_WIDTH), row_map),
            pl.BlockSpec((seg, GDN_WIDTH), row_map),
            pl.BlockSpec((seg, LANES), row_map),
            _const_spec((CONV_WIDTH, 3 * GDN_WIDTH)),
            _const_spec((1, LANES)),
            _const_spec((1, LANES)),
            _const_spec((1, GDN_HEAD_DIM)),
        ],
        out_specs=pl.BlockSpec((seg, GDN_WIDTH), row_map),
        out_shape=jax.ShapeDtypeStruct((t, GDN_WIDTH), F32),
        scratch_shapes=[
            pltpu.VMEM((GDN_HEADS, GDN_HEAD_DIM, GDN_HEAD_DIM), F32),
            pltpu.VMEM((CONV_TAIL + seg, 3 * GDN_WIDTH), F32),
            pltpu.VMEM((seg, 3 * GDN_WIDTH), F32),
        ],
        compiler_params=pltpu.CompilerParams(
            dimension_semantics=("arbitrary", "arbitrary"), vmem_limit_bytes=VMEM_LIMIT),
        name="gdn",
    )(qkv, z, ab, conv_w, a_log, dt_bias, out_gain)


_PAIR = LANES // MOBA_HEAD_DIM
_MOBA_STEP_HEADS = 8
_MOBA_TRIP_BLOCKS = 1
_VT_ROWS = MOBA_HEAD_DIM + 16
_LOG2E = math.log2(math.e)


def _norm_rope_pairs(xs, gain, cos_t, sin_t):
    d = MOBA_HEAD_DIM
    lane = lax.broadcasted_iota(jnp.int32, (1, LANES), 1)
    ri = lax.broadcasted_iota(jnp.int32, (LANES, LANES), 0)
    ci = lax.broadcasted_iota(jnp.int32, (LANES, LANES), 1)
    same_head = jnp.where((ri // d) == (ci // d), 1.0, 0.0).astype(BF16)
    parts = [_split(x * x) for x in xs]
    sums = [_bdot(hi, same_head) + _bdot(lo, same_head) for hi, lo in parts]
    half = ROPE_DIM // 2
    out = []
    for x, ssq in zip(xs, sums):
        xn = x * lax.rsqrt(ssq * (1.0 / d) + NORM_EPS) * gain
        partner = jnp.where((lane & (d - 1)) < half,
                            pltpu.roll(xn, LANES - half, axis=1),
                            pltpu.roll(xn, half, axis=1))
        out.append(xn * cos_t + partner * sin_t)
    return out


def _moba_kernel(q_ref, k_ref, v_ref, qg_ref, kg_ref, cq_ref, sq_ref, ck_ref, sk_ref, o_ref,
                 k_sc, vt_sc, km_sc, p_sc, *, n_blocks, n_heads):
    blk, d, nt = MOBA_BLOCK, MOBA_HEAD_DIM, _MOBA_TRIP_BLOCKS
    qb = pl.program_id(2)
    neg_inf = -jnp.inf
    heads = range(n_heads)
    pairs = range(n_heads // _PAIR)

    def slab(ref_rows, pr):
        return ref_rows[:, pr * LANES:(pr + 1) * LANES]

    @pl.when(qb == 0)
    def _prep_keys():
        extra = lax.broadcasted_iota(jnp.int32, (_VT_ROWS - d, blk), 0)
        ones_row = jnp.where(extra == 0, 1.0, 0.0).astype(BF16)
        for j in range(n_blocks):
            rows = slice(j * blk, (j + 1) * blk)
            k_rows = k_ref[0, rows, :]
            v_rows = v_ref[0, rows, :]
            kns = _norm_rope_pairs([slab(k_rows, pr).astype(F32) for pr in pairs], kg_ref[...], ck_ref[rows, :],
                                   sk_ref[rows, :])
            for pr in pairs:
                km_sc[pr, j:j + 1, :] = jnp.mean(kns[pr], axis=0, keepdims=True)
                k_sc[pr, j] = kns[pr].astype(BF16)
                vt = slab(v_rows, pr).astype(F32).T
                for h2 in range(_PAIR):
                    hh = pr * _PAIR + h2
                    vt_sc[hh, j, 0:d, :] = vt[h2 * d:(h2 + 1) * d, :].astype(BF16)
                    vt_sc[hh, j, d:, :] = ones_row

    blk_id = lax.broadcasted_iota(jnp.int32, (n_blocks, blk), 0)
    blk_f = blk_id.astype(F32)
    key_i = lax.broadcasted_iota(jnp.int32, (blk, blk), 0)
    qry_i = lax.broadcasted_iota(jnp.int32, (blk, blk), 1)
    dim_i = lax.broadcasted_iota(jnp.int32, (LANES, blk), 0)

    q_rows = q_ref[0]
    qt_pairs = [y.T for y in _norm_rope_pairs([slab(q_rows, pr) for pr in pairs], qg_ref[...], cq_ref[...],
                                              sq_ref[...])]
    qts = [jnp.where((dim_i // d) == hh % _PAIR, qt_pairs[hh // _PAIR], 0.0) for hh in heads]
    gates = [jnp.where(blk_id < qb, _dot3(km_sc[hh // _PAIR], qts[hh]), neg_inf) for hh in heads]
    picks = []
    for gate in gates:
        pk = []
        for r in range(MOBA_TOPK):
            top = jnp.max(gate, axis=0, keepdims=True)
            idx = jnp.min(jnp.where(gate == top, blk_f, float(n_blocks)), axis=0, keepdims=True)
            pk.append(jnp.where(r < qb, idx, -1.0))
            gate = jnp.where(blk_f == idx, neg_inf, gate)
        picks.append(pk)
    q_scaled = [(qt * (d ** -0.5 * _LOG2E)).astype(BF16) for qt in qts]
    s_own = [_bdot(k_sc[hh // _PAIR, qb], q_scaled[hh]) for hh in heads]
    stats0 = []
    for hh in heads:
        s_t = jnp.where(key_i <= qry_i, s_own[hh], neg_inf)
        m = jnp.max(s_t, axis=0, keepdims=True)
        p_sc[hh, 0] = jnp.exp2(s_t - m).astype(BF16)
        for a in range(1, nt):
            p_sc[hh, a] = jnp.zeros((blk, blk), BF16)
        stats0.append((m, jnp.zeros((_VT_ROWS, blk), F32)))

    def pending_block(t, a):
        return jnp.where(t == 0, qb, jnp.minimum((t - 1) * nt + a, n_blocks - 1))

    def pending_pv(t, hh):
        pv = None
        for a in range(nt):
            term = _bdot(vt_sc[hh, pending_block(t, a)], p_sc[hh, a])
            pv = term if pv is None else pv + term
        return pv

    def trip(t, carry):
        s_all = [[_bdot(k_sc[hh // _PAIR, jnp.minimum(t * nt + a, n_blocks - 1)], q_scaled[hh]) for a in range(nt)]
                 for hh in heads]
        pv_all = [pending_pv(t, hh) for hh in heads]
        out = []
        for hh in heads:
            m, acc = carry[hh]
            pk = picks[hh]
            chosen, m_new = [], m
            for a in range(nt):
                jf = (t * nt + a).astype(F32)
                chosen.append((pk[0] == jf) | (pk[1] == jf) | (pk[2] == jf))
                m_new = jnp.where(chosen[a], jnp.maximum(m_new, jnp.max(s_all[hh][a], axis=0, keepdims=True)), m_new)
            for a in range(nt):
                shift = jnp.where(chosen[a], m_new, jnp.inf)
                p_sc[hh, a] = jnp.exp2(s_all[hh][a] - shift).astype(BF16)
            out.append((m_new, jnp.exp2(m - m_new) * (acc + pv_all[hh])))
        return tuple(out)

    n_trips = (qb + nt - 1) // nt
    final = lax.fori_loop(0, n_trips, trip, tuple(stats0))
    outs = []
    for hh, (_, acc) in enumerate(final):
        acc = acc + pending_pv(n_trips, hh)
        outs.append(acc[:d] / acc[d:d + 1])
    o_ref[0] = jnp.concatenate(outs, axis=0).T


def _moba(mq, mk, mv, q_gain, k_gain, cos_t, sin_t, *, batch, seq):
    n_blocks = seq // MOBA_BLOCK
    n_heads = _MOBA_STEP_HEADS
    cols = n_heads * MOBA_HEAD_DIM
    shape3 = (batch, seq, MOBA_WIDTH)
    q_spec = pl.BlockSpec((1, MOBA_BLOCK, cols), lambda b, g, i: (b, i, g))
    kv_spec = pl.BlockSpec((1, seq, cols), lambda b, g, i: (b, 0, g))
    out = pl.pallas_call(
        functools.partial(_moba_kernel, n_blocks=n_blocks, n_heads=n_heads),
        grid=(batch, MOBA_HEADS // n_heads, n_blocks),
        in_specs=[
            q_spec, kv_spec, kv_spec,
            _const_spec((1, LANES)), _const_spec((1, LANES)),
            pl.BlockSpec((MOBA_BLOCK, LANES), lambda b, g, i: (i, 0)),
            pl.BlockSpec((MOBA_BLOCK, LANES), lambda b, g, i: (i, 0)),
            _const_spec((seq, LANES)), _const_spec((seq, LANES)),
        ],
        out_specs=q_spec,
        out_shape=jax.ShapeDtypeStruct(shape3, F32),
        scratch_shapes=[
            pltpu.VMEM((n_heads // _PAIR, n_blocks, MOBA_BLOCK, LANES), BF16),
            pltpu.VMEM((n_heads, n_blocks, _VT_ROWS, MOBA_BLOCK), BF16),
            pltpu.VMEM((n_heads // _PAIR, n_blocks, LANES), F32),
            pltpu.VMEM((n_heads, _MOBA_TRIP_BLOCKS, MOBA_BLOCK, MOBA_BLOCK), BF16),
        ],
        compiler_params=pltpu.CompilerParams(
            dimension_semantics=("arbitrary", "arbitrary", "arbitrary"), vmem_limit_bytes=VMEM_LIMIT),
        name="moba",
    )(mq.reshape(shape3), mk.reshape(shape3), mv.reshape(shape3), q_gain, k_gain, cos_t, sin_t, cos_t, sin_t)
    return out.reshape(batch * seq, MOBA_WIDTH)


def _rope_tables(seq):
    half = ROPE_DIM // 2
    inv_freq = jnp.power(jnp.float32(ROPE_THETA), -jnp.arange(half, dtype=F32) * 2.0 / ROPE_DIM)
    ang = jnp.arange(seq).astype(F32)[:, None] * inv_freq[None, :]
    cos, sin = jnp.cos(ang), jnp.sin(ang)
    rest = MOBA_HEAD_DIM - ROPE_DIM
    cos_h = jnp.concatenate([cos, cos, jnp.ones((seq, rest), F32)], axis=-1)
    sin_h = jnp.concatenate([-sin, sin, jnp.zeros((seq, rest), F32)], axis=-1)
    return jnp.tile(cos_h, (1, _PAIR)), jnp.tile(sin_h, (1, _PAIR))


def kernel(x, ffn1_norm, ffn1_w_gate, ffn1_w_up, ffn1_w_down, mix_norm, w_in, gdn_conv, gdn_a_log, gdn_dt_bias,
           gdn_out_norm, moba_q_norm, moba_k_norm, w_out, ffn2_norm, ffn2_w_gate, ffn2_w_up, ffn2_w_down):
    batch, seq, d_model = x.shape
    depth = w_in.shape[0]
    off_z = 3 * GDN_WIDTH
    off_ab = 4 * GDN_WIDTH
    off_moba = off_ab + 2 * GDN_HEADS
    cos_t, sin_t = _rope_tables(seq)
    xf = x.reshape(batch * seq, d_model)
    for l in range(depth):
        xf = _ffn(xf, ffn1_norm[l][None], ffn1_w_gate[l].astype(BF16), ffn1_w_up[l].astype(BF16),
                  ffn1_w_down[l].astype(BF16))
        w = w_in[l]
        w_ab = jnp.pad(w[:, off_ab:off_moba], ((0, 0), (0, LANES - 2 * GDN_HEADS)))
        w_all = jnp.concatenate([w[:, :off_ab], w_ab, w[:, off_moba:]], axis=1).astype(BF16)
        qkv, z, ab, mq, mk, mv = _inproj(xf, mix_norm[l][None], w_all)
        a_log = jnp.pad(gdn_a_log[l].astype(F32), (0, LANES - GDN_HEADS))[None]
        dt_b = jnp.pad(gdn_dt_bias[l].astype(F32), (0, LANES - GDN_HEADS))[None]
        o_gdn = _gdn(qkv, z, ab, gdn_conv[l], a_log, dt_b, gdn_out_norm[l][None], batch=batch, seq=seq)
        o_moba = _moba(mq, mk, mv, jnp.tile(moba_q_norm[l], _PAIR)[None], jnp.tile(moba_k_norm[l], _PAIR)[None],
                       cos_t, sin_t, batch=batch, seq=seq)
        wo = w_out[l].astype(BF16)
        xf = _ffn(xf, ffn2_norm[l][None], ffn2_w_gate[l].astype(BF16), ffn2_w_up[l].astype(BF16),
                  ffn2_w_down[l].astype(BF16), mix=(o_gdn, o_moba), mix_w=(wo[:GDN_WIDTH], wo[GDN_WIDTH:]))
    return xf.reshape(batch, seq, d_model)
```

```python
import functools
import itertools
import math

import jax
import jax.numpy as jnp
from jax import lax
from jax.experimental import pallas as pl
from jax.experimental.pallas import tpu as pltpu

F32 = jnp.float32
BF16 = jnp.bfloat16

NORM_EPS = 1e-6
GDN_HEADS = 4
GDN_HEAD_DIM = 128
GDN_WIDTH = GDN_HEADS * GDN_HEAD_DIM
GDN_CHUNK = 64
CONV_WIDTH = 4
MOBA_HEADS = 8
MOBA_HEAD_DIM = 64
MOBA_WIDTH = MOBA_HEADS * MOBA_HEAD_DIM
MOBA_BLOCK = 256
MOBA_TOPK = 3
ROPE_THETA = 500000.0
ROPE_DIM = MOBA_HEAD_DIM // 4
LANES = 128
MXU_TILE = 256
CONV_TAIL = 8
VMEM_LIMIT = 56 * 1024 * 1024

_NT = (((1,), (1,)), ((), ()))
_TN = (((0,), (0,)), ((), ()))


def _bdot(a, b, dims=None):
    a = a.astype(BF16)
    b = b.astype(BF16)
    if dims is None:
        return jnp.dot(a, b, preferred_element_type=F32)
    return lax.dot_general(a, b, dims, preferred_element_type=F32)


def _split(a):
    hi = a.astype(BF16)
    lo = (a - hi.astype(F32)).astype(BF16)
    return hi, lo


def _dot3(a, b, dims=None):
    a_hi, a_lo = _split(a)
    b_hi, b_lo = _split(b)
    return _bdot(a_hi, b_hi, dims) + (_bdot(a_hi, b_lo, dims) + _bdot(a_lo, b_hi, dims))


def _silu(x):
    return x * jax.nn.sigmoid(x)


def _softplus(x):
    return jnp.maximum(x, 0.0) + jnp.log1p(jnp.exp(-jnp.abs(x)))


def _rms_norm(x, g):
    return x * lax.rsqrt(jnp.mean(x * x, axis=-1, keepdims=True) + NORM_EPS) * g


def _ffn_kernel(*refs, n_split, n_mix):
    x_ref = refs[0]
    mix_refs = refs[1:1 + 2 * n_mix]
    g_ref, wg_ref, wu_ref, wd_ref, o_ref = refs[1 + 2 * n_mix:]
    x = x_ref[...]
    for i in range(n_mix):
        x = x + _bdot(mix_refs[i][...], mix_refs[n_mix + i][...])
    h = _rms_norm(x, g_ref[...]).astype(BF16)
    n_tiles = wg_ref.shape[1] // MXU_TILE
    cuts = [MXU_TILE * ((n_tiles * j + n_split - 1) // n_split) for j in range(n_split)] + [wg_ref.shape[1]]
    acc = None
    for lo, hi in zip(cuts[:-1], cuts[1:]):
        gate = jnp.dot(h, wg_ref[:, lo:hi], preferred_element_type=F32)
        up = jnp.dot(h, wu_ref[:, lo:hi], preferred_element_type=F32)
        act = (_silu(gate) * up).astype(BF16)
        y = jnp.dot(act, wd_ref[lo:hi, :], preferred_element_type=F32)
        acc = y if acc is None else acc + y
    o_ref[...] = x + 0.5 * acc


def _const_spec(shape):
    return pl.BlockSpec(shape, lambda *_: (0,) * len(shape), pipeline_mode=pl.Buffered(1))


def _ffn(x, norm_g, wg, wu, wd, mix=(), mix_w=(), *, tm=512, n_split=2):
    t, d = x.shape
    ff = wg.shape[1]
    row_spec = lambda width: pl.BlockSpec((tm, width), lambda i: (i, 0))
    return pl.pallas_call(
        functools.partial(_ffn_kernel, n_split=n_split, n_mix=len(mix)),
        grid=(t // tm,),
        in_specs=[row_spec(d)] + [row_spec(m.shape[1]) for m in mix] + [_const_spec(w.shape) for w in mix_w] + [
            _const_spec((1, d)),
            _const_spec((d, ff)),
            _const_spec((d, ff)),
            _const_spec((ff, d)),
        ],
        out_specs=row_spec(d),
        out_shape=jax.ShapeDtypeStruct((t, d), F32),
        compiler_params=pltpu.CompilerParams(dimension_semantics=("parallel",), vmem_limit_bytes=VMEM_LIMIT),
        name="ffn",
    )(x, *mix, *mix_w, norm_g, wg, wu, wd)


_IN_SPLITS = (3 * GDN_WIDTH, GDN_WIDTH, LANES, MOBA_WIDTH, MOBA_WIDTH, MOBA_WIDTH)
_IN_DTYPES = (F32, F32, F32, F32, F32, BF16)


def _inproj_kernel(x_ref, g_ref, w_ref, *out_refs):
    h = _rms_norm(x_ref[...], g_ref[...]).astype(BF16)
    p = jnp.dot(h, w_ref[...], preferred_element_type=F32)
    off = 0
    for ref, width in zip(out_refs, _IN_SPLITS):
        ref[...] = p[:, off:off + width].astype(ref.dtype)
        off += width


def _inproj(x, norm_g, w_all, *, tm=512):
    t, d = x.shape
    n = w_all.shape[1]
    return pl.pallas_call(
        _inproj_kernel,
        grid=(t // tm,),
        in_specs=[pl.BlockSpec((tm, d), lambda i: (i, 0)), _const_spec((1, d)), _const_spec((d, n))],
        out_specs=[pl.BlockSpec((tm, w), lambda i: (i, 0)) for w in _IN_SPLITS],
        out_shape=[jax.ShapeDtypeStruct((t, w), dt) for w, dt in zip(_IN_SPLITS, _IN_DTYPES)],
        compiler_params=pltpu.CompilerParams(dimension_semantics=("parallel",), vmem_limit_bytes=VMEM_LIMIT),
        name="inproj",
    )(x, norm_g, w_all)


_GDN_GROUP = 4
_BNT = (((2,), (2,)), ((0,), (0,)))
_BNN = (((2,), (1,)), ((0,), (0,)))
_BTN = (((1,), (1,)), ((0,), (0,)))


def _chunk_cumsum(x, c_len):
    pos = lax.broadcasted_iota(jnp.int32, x.shape, 0) & (c_len - 1)
    s = 1
    while s < c_len:
        x = x + jnp.where(pos >= s, pltpu.roll(x, s, axis=0), 0.0)
        s *= 2
    return x


def _unit_lower_inverse_rounds(lower, out, base=16):
    c = lower.shape[-1]
    ri = lax.broadcasted_iota(jnp.int32, (1, c, c), 1)
    ci = lax.broadcasted_iota(jnp.int32, (1, c, c), 2)
    same = lambda size: (ri // size) == (ci // size)
    neg = jnp.where(same(base), -lower, 0.0)
    inv = jnp.where(ri == ci, 1.0, 0.0) + neg
    pw = _bdot(neg, neg, _BNN)
    yield
    span = 2
    while 2 * span < base:
        step = _bdot(pw, inv, _BNN)
        pw = _bdot(pw, pw, _BNN)
        yield
        inv = inv + step
        span *= 2
    step = _bdot(pw, inv, _BNN)
    yield
    inv = inv + step
    size = base
    while size < c:
        off = jnp.where(same(size), 0.0, jnp.where(same(2 * size), lower, 0.0))
        left = _bdot(inv, off, _BNN)
        yield
        corr = _bdot(left, inv, _BNN)
        yield
        inv = inv - corr
        size *= 2
    out.append(inv)


def _gdn_kernel(qkv_ref, z_ref, ab_ref, cw_ref, alog_ref, dtb_ref, og_ref, o_ref, state_sc, xpad_sc, xc_sc,
                u_sc, w_sc, qd_sc, kd_sc, intra_sc, el_sc, *, n_groups):
    c_len, dh, n_heads, width, grp = GDN_CHUNK, GDN_HEAD_DIM, GDN_HEADS, GDN_WIDTH, _GDN_GROUP
    rows = grp * c_len
    nb = grp * n_heads

    seg = qkv_ref.shape[0]

    @pl.when(pl.program_id(1) == 0)
    def _reset():
        state_sc[...] = jnp.zeros_like(state_sc)
        xpad_sc[0:CONV_TAIL, :] = jnp.zeros((CONV_TAIL, xpad_sc.shape[1]), F32)

    @pl.when(pl.program_id(1) != 0)
    def _carry_history():
        xpad_sc[0:CONV_TAIL, :] = xpad_sc[seg:seg + CONV_TAIL, :]

    xpad_sc[CONV_TAIL:CONV_TAIL + seg, :] = qkv_ref[...]
    cw = cw_ref[...]
    for t in range(seg // rows):
        win = xpad_sc[t * rows:t * rows + CONV_TAIL + rows, :]
        conv = win * cw[CONV_WIDTH - 1:CONV_WIDTH]
        for back in range(1, CONV_WIDTH):
            conv = conv + pltpu.roll(win, back, axis=0) * cw[CONV_WIDTH - 1 - back:CONV_WIDTH - back]
        xc_sc[t * rows:(t + 1) * rows, :] = _silu(conv[CONV_TAIL:])

    ri = lax.broadcasted_iota(jnp.int32, (1, c_len, c_len), 1)
    ci = lax.broadcasted_iota(jnp.int32, (1, c_len, c_len), 2)
    causal = ri >= ci
    strict = ri > ci
    og = og_ref[...]

    def tiles(x, off):
        return jnp.stack([x[g * c_len:(g + 1) * c_len, off + h * dh:off + (h + 1) * dh]
                          for g in range(grp) for h in range(n_heads)])

    def lane_tiles(x, lane0):
        return jnp.stack([jnp.broadcast_to(x[g * c_len:(g + 1) * c_len, lane0 + h:lane0 + h + 1], (c_len, dh))
                          for g in range(grp) for h in range(n_heads)])

    def local_rounds(gi):
        r0 = gi * rows if isinstance(gi, int) else pl.multiple_of(gi * rows, rows)
        xc = xc_sc[pl.ds(r0, rows), :]
        ab = ab_ref[pl.ds(r0, rows), :]
        g_tok = -jnp.exp(alog_ref[...]) * _softplus(ab + dtb_ref[...])
        gc_tok = _chunk_cumsum(g_tok, c_len)
        gc = lane_tiles(gc_tok, 0)
        gc_t = gc_tok.T
        gc_row = jnp.stack([jnp.broadcast_to(gc_t[h:h + 1, g * c_len:(g + 1) * c_len], (c_len, c_len))
                            for g in range(grp) for h in range(n_heads)])
        beta = lane_tiles(jax.nn.sigmoid(ab), n_heads)
        q = tiles(xc, 0)
        k = tiles(xc, width)
        v = tiles(xc, 2 * width)
        q = q * lax.rsqrt(jnp.sum(q * q, axis=-1, keepdims=True) + NORM_EPS) * (dh ** -0.5)
        k = k * lax.rsqrt(jnp.sum(k * k, axis=-1, keepdims=True) + NORM_EPS)
        g_last = gc[:, c_len - 1:c_len, :]
        e_gc = jnp.exp(gc)
        decay = jnp.exp(jnp.where(causal, gc[:, :, :c_len] - gc_row, -jnp.inf))
        k_beta = k * beta
        kk = _bdot(k_beta, k, _BNT)
        qk = _bdot(q, k, _BNT)
        yield
        lower = jnp.where(strict, kk * decay, 0.0)
        inv_out = []
        yield from _unit_lower_inverse_rounds(lower, inv_out)
        sol = _bdot(inv_out[0], jnp.concatenate([v * beta, k_beta * e_gc], axis=-1), _BNN)
        yield
        u_sc[...] = sol[:, :, :dh]
        w_sc[...] = sol[:, :, dh:].astype(BF16)
        qd_sc[...] = (q * e_gc).astype(BF16)
        kd_sc[...] = (k * jnp.exp(g_last - gc)).astype(BF16)
        intra_sc[...] = (qk * decay).astype(BF16)
        el_sc[...] = jnp.exp(g_last)

    def recurrence_rounds(gi):
        r0 = gi * rows if isinstance(gi, int) else pl.multiple_of(gi * rows, rows)
        state = state_sc[...]
        for g in range(grp):
            sl = slice(g * n_heads, (g + 1) * n_heads)
            w_s = _bdot(w_sc[sl], state, _BNN)
            q_s = _bdot(qd_sc[sl], state, _BNN)
            yield
            v_new = u_sc[sl] - w_s
            a_v = _bdot(intra_sc[sl], v_new, _BNN)
            k_v = _bdot(kd_sc[sl], v_new, _BTN)
            yield
            state = state * el_sc[sl] + k_v
            on = _rms_norm(q_s + a_v, og)
            zz = _silu(z_ref[pl.ds(r0 + g * c_len, c_len), :])
            o_ref[pl.ds(r0 + g * c_len, c_len), :] = jnp.concatenate([on[h] for h in range(n_heads)], axis=-1) * zz
        state_sc[...] = state

    def interleave(*generators):
        for _ in itertools.zip_longest(*generators):
            pass

    interleave(local_rounds(0))

    def pipelined(gi, carry):
        interleave(local_rounds(gi), recurrence_rounds(gi - 1))
        return carry

    lax.fori_loop(1, n_groups, pipelined, 0)
    interleave(recurrence_rounds(n_groups - 1))


def _gdn(qkv, z, ab, conv_w, a_log, dt_bias, out_gain, *, batch, seq, seg=1024):
    t = batch * seq
    n_seg = seq // seg
    n_tiles = _GDN_GROUP * GDN_HEADS
    row_map = lambda b, s: (b * n_seg + s, 0)
    return pl.pallas_call(
        functools.partial(_gdn_kernel, n_groups=seg // (GDN_CHUNK * _GDN_GROUP)),
        grid=(batch, n_seg),
        in_specs=[
            pl.BlockSpec((seg, 3 * GDN_WIDTH), row_map),
            pl.BlockSpec((seg, GDN_WIDTH), row_map),
            pl.BlockSpec((seg, LANES), row_map),
            _const_spec((CONV_WIDTH, 3 * GDN_WIDTH)),
            _const_spec((1, LANES)),
            _const_spec((1, LANES)),
            _const_spec((1, GDN_HEAD_DIM)),
        ],
        out_specs=pl.BlockSpec((seg, GDN_WIDTH), row_map),
        out_shape=jax.ShapeDtypeStruct((t, GDN_WIDTH), F32),
        scratch_shapes=[
            pltpu.VMEM((GDN_HEADS, GDN_HEAD_DIM, GDN_HEAD_DIM), F32),
            pltpu.VMEM((CONV_TAIL + seg, 3 * GDN_WIDTH), F32),
            pltpu.VMEM((seg, 3 * GDN_WIDTH), F32),
            pltpu.VMEM((n_tiles, GDN_CHUNK, GDN_HEAD_DIM), F32),
            pltpu.VMEM((n_tiles, GDN_CHUNK, GDN_HEAD_DIM), BF16),
            pltpu.VMEM((n_tiles, GDN_CHUNK, GDN_HEAD_DIM), BF16),
            pltpu.VMEM((n_tiles, GDN_CHUNK, GDN_HEAD_DIM), BF16),
            pltpu.VMEM((n_tiles, GDN_CHUNK, GDN_CHUNK), BF16),
            pltpu.VMEM((n_tiles, 1, GDN_HEAD_DIM), F32),
        ],
        compiler_params=pltpu.CompilerParams(
            dimension_semantics=("arbitrary", "arbitrary"), vmem_limit_bytes=VMEM_LIMIT),
        name="gdn",
    )(qkv, z, ab, conv_w, a_log, dt_bias, out_gain)


_PAIR = LANES // MOBA_HEAD_DIM
_MOBA_STEP_HEADS = 8
_MOBA_TRIP_BLOCKS = 1
_VT_ROWS = MOBA_HEAD_DIM + 16
_LOG2E = math.log2(math.e)


def _norm_rope_pairs(xs, gain, cos_t, sin_t):
    d = MOBA_HEAD_DIM
    lane = lax.broadcasted_iota(jnp.int32, (1, LANES), 1)
    ri = lax.broadcasted_iota(jnp.int32, (LANES, LANES), 0)
    ci = lax.broadcasted_iota(jnp.int32, (LANES, LANES), 1)
    same_head = jnp.where((ri // d) == (ci // d), 1.0, 0.0).astype(BF16)
    parts = [_split(x * x) for x in xs]
    sums = [_bdot(hi, same_head) + _bdot(lo, same_head) for hi, lo in parts]
    half = ROPE_DIM // 2
    out = []
    for x, ssq in zip(xs, sums):
        xn = x * lax.rsqrt(ssq * (1.0 / d) + NORM_EPS) * gain
        partner = jnp.where((lane & (d - 1)) < half,
                            pltpu.roll(xn, LANES - half, axis=1),
                            pltpu.roll(xn, half, axis=1))
        out.append(xn * cos_t + partner * sin_t)
    return out


def _moba_kernel(q_ref, k_ref, v_ref, qg_ref, kg_ref, cq_ref, sq_ref, ck_ref, sk_ref, o_ref,
                 k_sc, vt_sc, km_sc, p_sc, *, n_blocks, n_heads):
    blk, d, nt = MOBA_BLOCK, MOBA_HEAD_DIM, _MOBA_TRIP_BLOCKS
    qb = pl.program_id(2)
    neg_inf = -jnp.inf
    heads = range(n_heads)
    pairs = range(n_heads // _PAIR)

    def slab(ref_rows, pr):
        return ref_rows[:, pr * LANES:(pr + 1) * LANES]

    @pl.when(qb == 0)
    def _prep_keys():
        extra = lax.broadcasted_iota(jnp.int32, (_VT_ROWS - d, blk), 0)
        ones_row = jnp.where(extra == 0, 1.0, 0.0).astype(BF16)
        for j in range(n_blocks):
            rows = slice(j * blk, (j + 1) * blk)
            k_rows = k_ref[0, rows, :]
            v_rows = v_ref[0, rows, :]
            kns = _norm_rope_pairs([slab(k_rows, pr).astype(F32) for pr in pairs], kg_ref[...], ck_ref[rows, :],
                                   sk_ref[rows, :])
            for pr in pairs:
                km_sc[pr, j:j + 1, :] = jnp.mean(kns[pr], axis=0, keepdims=True)
                k_sc[pr, j] = kns[pr].astype(BF16)
                vt = slab(v_rows, pr).astype(F32).T
                for h2 in range(_PAIR):
                    hh = pr * _PAIR + h2
                    vt_sc[hh, j, 0:d, :] = vt[h2 * d:(h2 + 1) * d, :].astype(BF16)
                    vt_sc[hh, j, d:, :] = ones_row

    blk_id = lax.broadcasted_iota(jnp.int32, (n_blocks, blk), 0)
    blk_f = blk_id.astype(F32)
    key_i = lax.broadcasted_iota(jnp.int32, (blk, blk), 0)
    qry_i = lax.broadcasted_iota(jnp.int32, (blk, blk), 1)
    dim_i = lax.broadcasted_iota(jnp.int32, (LANES, blk), 0)

    q_rows = q_ref[0]
    qt_pairs = [y.T for y in _norm_rope_pairs([slab(q_rows, pr) for pr in pairs], qg_ref[...], cq_ref[...],
                                              sq_ref[...])]
    qts = [jnp.where((dim_i // d) == hh % _PAIR, qt_pairs[hh // _PAIR], 0.0) for hh in heads]
    gates = [jnp.where(blk_id < qb, _dot3(km_sc[hh // _PAIR], qts[hh]), neg_inf) for hh in heads]
    picks = []
    for gate in gates:
        pk = []
        for r in range(MOBA_TOPK):
            top = jnp.max(gate, axis=0, keepdims=True)
            idx = jnp.min(jnp.where(gate == top, blk_f, float(n_blocks)), axis=0, keepdims=True)
            pk.append(jnp.where(r < qb, idx, -1.0))
            gate = jnp.where(blk_f == idx, neg_inf, gate)
        picks.append(pk)
    q_scaled = [(qt * (d ** -0.5 * _LOG2E)).astype(BF16) for qt in qts]
    s_own = [_bdot(k_sc[hh // _PAIR, qb], q_scaled[hh]) for hh in heads]
    stats0 = []
    for hh in heads:
        s_t = jnp.where(key_i <= qry_i, s_own[hh], neg_inf)
        m = jnp.max(s_t, axis=0, keepdims=True)
        p_sc[hh, 0] = jnp.exp2(s_t - m).astype(BF16)
        for a in range(1, nt):
            p_sc[hh, a] = jnp.zeros((blk, blk), BF16)
        stats0.append((m, jnp.zeros((_VT_ROWS, blk), F32)))

    def pending_block(t, a):
        return jnp.where(t == 0, qb, jnp.minimum((t - 1) * nt + a, n_blocks - 1))

    def pending_pv(t, hh):
        pv = None
        for a in range(nt):
            term = _bdot(vt_sc[hh, pending_block(t, a)], p_sc[hh, a])
            pv = term if pv is None else pv + term
        return pv

    def trip(t, carry):
        s_all = [[_bdot(k_sc[hh // _PAIR, jnp.minimum(t * nt + a, n_blocks - 1)], q_scaled[hh]) for a in range(nt)]
                 for hh in heads]
        pv_all = [pending_pv(t, hh) for hh in heads]
        out = []
        for hh in heads:
            m, acc = carry[hh]
            pk = picks[hh]
            chosen, m_new = [], m
            for a in range(nt):
                jf = (t * nt + a).astype(F32)
                chosen.append((pk[0] == jf) | (pk[1] == jf) | (pk[2] == jf))
                m_new = jnp.where(chosen[a], jnp.maximum(m_new, jnp.max(s_all[hh][a], axis=0, keepdims=True)), m_new)
            for a in range(nt):
                shift = jnp.where(chosen[a], m_new, jnp.inf)
                p_sc[hh, a] = jnp.exp2(s_all[hh][a] - shift).astype(BF16)
            out.append((m_new, jnp.exp2(m - m_new) * (acc + pv_all[hh])))
        return tuple(out)

    n_trips = (qb + nt - 1) // nt
    final = lax.fori_loop(0, n_trips, trip, tuple(stats0))
    outs = []
    for hh, (_, acc) in enumerate(final):
        acc = acc + pending_pv(n_trips, hh)
        outs.append(acc[:d] / acc[d:d + 1])
    o_ref[0] = jnp.concatenate(outs, axis=0).T


def _moba(mq, mk, mv, q_gain, k_gain, cos_t, sin_t, *, batch, seq):
    n_blocks = seq // MOBA_BLOCK
    n_heads = _MOBA_STEP_HEADS
    cols = n_heads * MOBA_HEAD_DIM
    shape3 = (batch, seq, MOBA_WIDTH)
    q_spec = pl.BlockSpec((1, MOBA_BLOCK, cols), lambda b, g, i: (b, i, g))
    kv_spec = pl.BlockSpec((1, seq, cols), lambda b, g, i: (b, 0, g))
    out = pl.pallas_call(
        functools.partial(_moba_kernel, n_blocks=n_blocks, n_heads=n_heads),
        grid=(batch, MOBA_HEADS // n_heads, n_blocks),
        in_specs=[
            q_spec, kv_spec, kv_spec,
            _const_spec((1, LANES)), _const_spec((1, LANES)),
            pl.BlockSpec((MOBA_BLOCK, LANES), lambda b, g, i: (i, 0)),
            pl.BlockSpec((MOBA_BLOCK, LANES), lambda b, g, i: (i, 0)),
            _const_spec((seq, LANES)), _const_spec((seq, LANES)),
        ],
        out_specs=q_spec,
        out_shape=jax.ShapeDtypeStruct(shape3, F32),
        scratch_shapes=[
            pltpu.VMEM((n_heads // _PAIR, n_blocks, MOBA_BLOCK, LANES), BF16),
            pltpu.VMEM((n_heads, n_blocks, _VT_ROWS, MOBA_BLOCK), BF16),
            pltpu.VMEM((n_heads // _PAIR, n_blocks, LANES), F32),
            pltpu.VMEM((n_heads, _MOBA_TRIP_BLOCKS, MOBA_BLOCK, MOBA_BLOCK), BF16),
        ],
        compiler_params=pltpu.CompilerParams(
            dimension_semantics=("arbitrary", "arbitrary", "arbitrary"), vmem_limit_bytes=VMEM_LIMIT),
        name="moba",
    )(mq.reshape(shape3), mk.reshape(shape3), mv.reshape(shape3), q_gain, k_gain, cos_t, sin_t, cos_t, sin_t)
    return out.reshape(batch * seq, MOBA_WIDTH)


def _rope_tables(seq):
    half = ROPE_DIM // 2
    inv_freq = jnp.power(jnp.float32(ROPE_THETA), -jnp.arange(half, dtype=F32) * 2.0 / ROPE_DIM)
    ang = jnp.arange(seq).astype(F32)[:, None] * inv_freq[None, :]
    cos, sin = jnp.cos(ang), jnp.sin(ang)
    rest = MOBA_HEAD_DIM - ROPE_DIM
    cos_h = jnp.concatenate([cos, cos, jnp.ones((seq, rest), F32)], axis=-1)
    sin_h = jnp.concatenate([-sin, sin, jnp.zeros((seq, rest), F32)], axis=-1)
    return jnp.tile(cos_h, (1, _PAIR)), jnp.tile(sin_h, (1, _PAIR))


def kernel(x, ffn1_norm, ffn1_w_gate, ffn1_w_up, ffn1_w_down, mix_norm, w_in, gdn_conv, gdn_a_log, gdn_dt_bias,
           gdn_out_norm, moba_q_norm, moba_k_norm, w_out, ffn2_norm, ffn2_w_gate, ffn2_w_up, ffn2_w_down):
    batch, seq, d_model = x.shape
    depth = w_in.shape[0]
    off_z = 3 * GDN_WIDTH
    off_ab = 4 * GDN_WIDTH
    off_moba = off_ab + 2 * GDN_HEADS
    cos_t, sin_t = _rope_tables(seq)
    xf = x.reshape(batch * seq, d_model)
    for l in range(depth):
        xf = _ffn(xf, ffn1_norm[l][None], ffn1_w_gate[l].astype(BF16), ffn1_w_up[l].astype(BF16),
                  ffn1_w_down[l].astype(BF16))
        w = w_in[l]
        w_ab = jnp.pad(w[:, off_ab:off_moba], ((0, 0), (0, LANES - 2 * GDN_HEADS)))
        w_all = jnp.concatenate([w[:, :off_ab], w_ab, w[:, off_moba:]], axis=1).astype(BF16)
        qkv, z, ab, mq, mk, mv = _inproj(xf, mix_norm[l][None], w_all)
        a_log = jnp.pad(gdn_a_log[l].astype(F32), (0, LANES - GDN_HEADS))[None]
        dt_b = jnp.pad(gdn_dt_bias[l].astype(F32), (0, LANES - GDN_HEADS))[None]
        o_gdn = _gdn(qkv, z, ab, gdn_conv[l], a_log, dt_b, gdn_out_norm[l][None], batch=batch, seq=seq)
        o_moba = _moba(mq, mk, mv, jnp.tile(moba_q_norm[l], _PAIR)[None], jnp.tile(moba_k_norm[l], _PAIR)[None],
                       cos_t, sin_t, batch=batch, seq=seq)
        wo = w_out[l].astype(BF16)
        xf = _ffn(xf, ffn2_norm[l][None], ffn2_w_gate[l].astype(BF16), ffn2_w_up[l].astype(BF16),
                  ffn2_w_down[l].astype(BF16), mix=(o_gdn, o_moba), mix_w=(wo[:GDN_WIDTH], wo[GDN_WIDTH:]))
    return xf.reshape(batch, seq, d_model)
```

```python
import functools
import itertools
import math

import jax
import jax.numpy as jnp
from jax import lax
from jax.experimental import pallas as pl
from jax.experimental.pallas import tpu as pltpu

F32 = jnp.float32
BF16 = jnp.bfloat16

NORM_EPS = 1e-6
GDN_HEADS = 4
GDN_HEAD_DIM = 128
GDN_WIDTH = GDN_HEADS * GDN_HEAD_DIM
GDN_CHUNK = 64
CONV_WIDTH = 4
MOBA_HEADS = 8
MOBA_HEAD_DIM = 64
MOBA_WIDTH = MOBA_HEADS * MOBA_HEAD_DIM
MOBA_BLOCK = 256
MOBA_TOPK = 3
ROPE_THETA = 500000.0
ROPE_DIM = MOBA_HEAD_DIM // 4
LANES = 128
MXU_TILE = 256
CONV_TAIL = 8
VMEM_LIMIT = 56 * 1024 * 1024

_NT = (((1,), (1,)), ((), ()))
_TN = (((0,), (0,)), ((), ()))


def _bdot(a, b, dims=None):
    a = a.astype(BF16)
    b = b.astype(BF16)
    if dims is None:
        return jnp.dot(a, b, preferred_element_type=F32)
    return lax.dot_general(a, b, dims, preferred_element_type=F32)


def _split(a):
    hi = a.astype(BF16)
    lo = (a - hi.astype(F32)).astype(BF16)
    return hi, lo


def _dot3(a, b, dims=None):
    a_hi, a_lo = _split(a)
    b_hi, b_lo = _split(b)
    return _bdot(a_hi, b_hi, dims) + (_bdot(a_hi, b_lo, dims) + _bdot(a_lo, b_hi, dims))


def _silu(x):
    return x * jax.nn.sigmoid(x)


def _softplus(x):
    return jnp.maximum(x, 0.0) + jnp.log1p(jnp.exp(-jnp.abs(x)))


def _rms_norm(x, g):
    return x * lax.rsqrt(jnp.mean(x * x, axis=-1, keepdims=True) + NORM_EPS) * g


def _ffn_kernel(*refs, n_split, n_mix):
    x_ref = refs[0]
    mix_refs = refs[1:1 + 2 * n_mix]
    g_ref, wg_ref, wu_ref, wd_ref, o_ref = refs[1 + 2 * n_mix:]
    x = x_ref[...]
    for i in range(n_mix):
        x = x + _bdot(mix_refs[i][...], mix_refs[n_mix + i][...])
    h = _rms_norm(x, g_ref[...]).astype(BF16)
    n_tiles = wg_ref.shape[1] // MXU_TILE
    cuts = [MXU_TILE * ((n_tiles * j + n_split - 1) // n_split) for j in range(n_split)] + [wg_ref.shape[1]]
    acc = None
    for lo, hi in zip(cuts[:-1], cuts[1:]):
        gate = jnp.dot(h, wg_ref[:, lo:hi], preferred_element_type=F32)
        up = jnp.dot(h, wu_ref[:, lo:hi], preferred_element_type=F32)
        act = (_silu(gate) * up).astype(BF16)
        y = jnp.dot(act, wd_ref[lo:hi, :], preferred_element_type=F32)
        acc = y if acc is None else acc + y
    o_ref[...] = x + 0.5 * acc


def _const_spec(shape):
    return pl.BlockSpec(shape, lambda *_: (0,) * len(shape), pipeline_mode=pl.Buffered(1))


def _ffn(x, norm_g, wg, wu, wd, mix=(), mix_w=(), *, tm=512, n_split=2):
    t, d = x.shape
    ff = wg.shape[1]
    row_spec = lambda width: pl.BlockSpec((tm, width), lambda i: (i, 0))
    return pl.pallas_call(
        functools.partial(_ffn_kernel, n_split=n_split, n_mix=len(mix)),
        grid=(t // tm,),
        in_specs=[row_spec(d)] + [row_spec(m.shape[1]) for m in mix] + [_const_spec(w.shape) for w in mix_w] + [
            _const_spec((1, d)),
            _const_spec((d, ff)),
            _const_spec((d, ff)),
            _const_spec((ff, d)),
        ],
        out_specs=row_spec(d),
        out_shape=jax.ShapeDtypeStruct((t, d), F32),
        compiler_params=pltpu.CompilerParams(dimension_semantics=("parallel",), vmem_limit_bytes=VMEM_LIMIT),
        name="ffn",
    )(x, *mix, *mix_w, norm_g, wg, wu, wd)


_IN_SPLITS = (3 * GDN_WIDTH, GDN_WIDTH, LANES, MOBA_WIDTH, MOBA_WIDTH, MOBA_WIDTH)
_IN_DTYPES = (F32, F32, F32, F32, F32, BF16)


def _inproj_kernel(x_ref, g_ref, w_ref, cw_ref, *refs, tiles_per_seq):
    *out_refs, raw_sc = refs
    i = pl.program_id(0)
    tm = x_ref.shape[0]
    conv_cols = _IN_SPLITS[0]

    @pl.when(i == 0)
    def _nothing_pending():
        raw_sc[...] = jnp.zeros_like(raw_sc)

    win = raw_sc[...]
    cw = cw_ref[...]
    conv = win * cw[CONV_WIDTH - 1:CONV_WIDTH]
    for back in range(1, CONV_WIDTH):
        conv = conv + pltpu.roll(win, back, axis=0) * cw[CONV_WIDTH - 1 - back:CONV_WIDTH - back]
    out_refs[0][...] = _silu(conv[CONV_TAIL:])

    h = _rms_norm(x_ref[...], g_ref[...]).astype(BF16)
    p = jnp.dot(h, w_ref[...], preferred_element_type=F32)
    new_sequence = (i % tiles_per_seq) == 0
    raw_sc[0:CONV_TAIL, :] = jnp.where(new_sequence, 0.0, win[tm:])
    raw_sc[CONV_TAIL:, :] = p[:, :conv_cols]
    off = conv_cols
    for ref, width in zip(out_refs[1:], _IN_SPLITS[1:]):
        ref[...] = p[:, off:off + width].astype(ref.dtype)
        off += width


def _inproj(x, norm_g, w_all, conv_w, *, seq, tm=512):
    t, d = x.shape
    n = w_all.shape[1]
    last = t // tm - 1
    cur_map = lambda i: (jnp.minimum(i, last), 0)
    lag_map = lambda i: (jnp.maximum(i - 1, 0), 0)
    return pl.pallas_call(
        functools.partial(_inproj_kernel, tiles_per_seq=seq // tm),
        grid=(t // tm + 1,),
        in_specs=[pl.BlockSpec((tm, d), cur_map), _const_spec((1, d)), _const_spec((d, n)),
                  _const_spec(conv_w.shape)],
        out_specs=[pl.BlockSpec((tm, _IN_SPLITS[0]), lag_map)] + [pl.BlockSpec((tm, w), cur_map) for w in _IN_SPLITS[1:]],
        out_shape=[jax.ShapeDtypeStruct((t, w), dt) for w, dt in zip(_IN_SPLITS, _IN_DTYPES)],
        scratch_shapes=[pltpu.VMEM((CONV_TAIL + tm, _IN_SPLITS[0]), F32)],
        compiler_params=pltpu.CompilerParams(dimension_semantics=("arbitrary",), vmem_limit_bytes=VMEM_LIMIT),
        name="inproj",
    )(x, norm_g, w_all, conv_w)


_GDN_GROUP = 4
_BNT = (((2,), (2,)), ((0,), (0,)))
_BNN = (((2,), (1,)), ((0,), (0,)))
_BTN = (((1,), (1,)), ((0,), (0,)))


def _chunk_cumsum(x, c_len):
    pos = lax.broadcasted_iota(jnp.int32, x.shape, 0) & (c_len - 1)
    s = 1
    while s < c_len:
        x = x + jnp.where(pos >= s, pltpu.roll(x, s, axis=0), 0.0)
        s *= 2
    return x


def _unit_lower_inverse_rounds(lower, out, base=16):
    c = lower.shape[-1]
    ri = lax.broadcasted_iota(jnp.int32, (1, c, c), 1)
    ci = lax.broadcasted_iota(jnp.int32, (1, c, c), 2)
    same = lambda size: (ri // size) == (ci // size)
    neg = jnp.where(same(base), -lower, 0.0)
    inv = jnp.where(ri == ci, 1.0, 0.0) + neg
    pw = _bdot(neg, neg, _BNN)
    yield
    span = 2
    while 2 * span < base:
        step = _bdot(pw, inv, _BNN)
        pw = _bdot(pw, pw, _BNN)
        yield
        inv = inv + step
        span *= 2
    step = _bdot(pw, inv, _BNN)
    yield
    inv = inv + step
    size = base
    while size < c:
        off = jnp.where(same(size), 0.0, jnp.where(same(2 * size), lower, 0.0))
        left = _bdot(inv, off, _BNN)
        yield
        corr = _bdot(left, inv, _BNN)
        yield
        inv = inv - corr
        size *= 2
    out.append(inv)


def _gdn_kernel(xc_ref, z_ref, ab_ref, alog_ref, dtb_ref, og_ref, o_ref, state_sc,
                u_sc, w_sc, qd_sc, kd_sc, intra_sc, el_sc, *, n_groups):
    c_len, dh, n_heads, width, grp = GDN_CHUNK, GDN_HEAD_DIM, GDN_HEADS, GDN_WIDTH, _GDN_GROUP
    rows = grp * c_len
    nb = grp * n_heads

    @pl.when(pl.program_id(1) == 0)
    def _reset():
        state_sc[...] = jnp.zeros_like(state_sc)

    ri = lax.broadcasted_iota(jnp.int32, (1, c_len, c_len), 1)
    ci = lax.broadcasted_iota(jnp.int32, (1, c_len, c_len), 2)
    causal = ri >= ci
    strict = ri > ci
    og = og_ref[...]

    def tiles(x, off):
        return jnp.stack([x[g * c_len:(g + 1) * c_len, off + h * dh:off + (h + 1) * dh]
                          for g in range(grp) for h in range(n_heads)])

    def lane_tiles(x, lane0):
        return jnp.stack([jnp.broadcast_to(x[g * c_len:(g + 1) * c_len, lane0 + h:lane0 + h + 1], (c_len, dh))
                          for g in range(grp) for h in range(n_heads)])

    def local_rounds(gi):
        r0 = gi * rows if isinstance(gi, int) else pl.multiple_of(gi * rows, rows)
        xc = xc_ref[pl.ds(r0, rows), :]
        ab = ab_ref[pl.ds(r0, rows), :]
        g_tok = -jnp.exp(alog_ref[...]) * _softplus(ab + dtb_ref[...])
        gc_tok = _chunk_cumsum(g_tok, c_len)
        gc = lane_tiles(gc_tok, 0)
        gc_t = gc_tok.T
        gc_row = jnp.stack([jnp.broadcast_to(gc_t[h:h + 1, g * c_len:(g + 1) * c_len], (c_len, c_len))
                            for g in range(grp) for h in range(n_heads)])
        beta = lane_tiles(jax.nn.sigmoid(ab), n_heads)
        q = tiles(xc, 0)
        k = tiles(xc, width)
        v = tiles(xc, 2 * width)
        q = q * lax.rsqrt(jnp.sum(q * q, axis=-1, keepdims=True) + NORM_EPS) * (dh ** -0.5)
        k = k * lax.rsqrt(jnp.sum(k * k, axis=-1, keepdims=True) + NORM_EPS)
        g_last = gc[:, c_len - 1:c_len, :]
        e_gc = jnp.exp(gc)
        decay = jnp.exp(jnp.where(causal, gc[:, :, :c_len] - gc_row, -jnp.inf))
        k_beta = k * beta
        kk = _bdot(k_beta, k, _BNT)
        qk = _bdot(q, k, _BNT)
        yield
        lower = jnp.where(strict, kk * decay, 0.0)
        inv_out = []
        yield from _unit_lower_inverse_rounds(lower, inv_out)
        sol = _bdot(inv_out[0], jnp.concatenate([v * beta, k_beta * e_gc], axis=-1), _BNN)
        yield
        u_sc[...] = sol[:, :, :dh]
        w_sc[...] = sol[:, :, dh:].astype(BF16)
        qd_sc[...] = (q * e_gc).astype(BF16)
        kd_sc[...] = (k * jnp.exp(g_last - gc)).astype(BF16)
        intra_sc[...] = (qk * decay).astype(BF16)
        el_sc[...] = jnp.exp(g_last)

    def recurrence_rounds(gi):
        r0 = gi * rows if isinstance(gi, int) else pl.multiple_of(gi * rows, rows)
        state = state_sc[...]
        for g in range(grp):
            sl = slice(g * n_heads, (g + 1) * n_heads)
            w_s = _bdot(w_sc[sl], state, _BNN)
            q_s = _bdot(qd_sc[sl], state, _BNN)
            yield
            v_new = u_sc[sl] - w_s
            a_v = _bdot(intra_sc[sl], v_new, _BNN)
            k_v = _bdot(kd_sc[sl], v_new, _BTN)
            yield
            state = state * el_sc[sl] + k_v
            on = _rms_norm(q_s + a_v, og)
            zz = _silu(z_ref[pl.ds(r0 + g * c_len, c_len), :])
            o_ref[pl.ds(r0 + g * c_len, c_len), :] = jnp.concatenate([on[h] for h in range(n_heads)], axis=-1) * zz
        state_sc[...] = state

    def interleave(*generators):
        for _ in itertools.zip_longest(*generators):
            pass

    interleave(local_rounds(0))

    def pipelined(gi, carry):
        interleave(local_rounds(gi), recurrence_rounds(gi - 1))
        return carry

    lax.fori_loop(1, n_groups, pipelined, 0)
    interleave(recurrence_rounds(n_groups - 1))


def _gdn(xc, z, ab, a_log, dt_bias, out_gain, *, batch, seq, seg=2048):
    t = batch * seq
    n_seg = seq // seg
    n_tiles = _GDN_GROUP * GDN_HEADS
    row_map = lambda b, s: (b * n_seg + s, 0)
    return pl.pallas_call(
        functools.partial(_gdn_kernel, n_groups=seg // (GDN_CHUNK * _GDN_GROUP)),
        grid=(batch, n_seg),
        in_specs=[
            pl.BlockSpec((seg, 3 * GDN_WIDTH), row_map),
            pl.BlockSpec((seg, GDN_WIDTH), row_map),
            pl.BlockSpec((seg, LANES), row_map),
            _const_spec((1, LANES)),
            _const_spec((1, LANES)),
            _const_spec((1, GDN_HEAD_DIM)),
        ],
        out_specs=pl.BlockSpec((seg, GDN_WIDTH), row_map),
        out_shape=jax.ShapeDtypeStruct((t, GDN_WIDTH), F32),
        scratch_shapes=[
            pltpu.VMEM((GDN_HEADS, GDN_HEAD_DIM, GDN_HEAD_DIM), F32),
            pltpu.VMEM((n_tiles, GDN_CHUNK, GDN_HEAD_DIM), F32),
            pltpu.VMEM((n_tiles, GDN_CHUNK, GDN_HEAD_DIM), BF16),
            pltpu.VMEM((n_tiles, GDN_CHUNK, GDN_HEAD_DIM), BF16),
            pltpu.VMEM((n_tiles, GDN_CHUNK, GDN_HEAD_DIM), BF16),
            pltpu.VMEM((n_tiles, GDN_CHUNK, GDN_CHUNK), BF16),
            pltpu.VMEM((n_tiles, 1, GDN_HEAD_DIM), F32),
        ],
        compiler_params=pltpu.CompilerParams(
            dimension_semantics=("arbitrary", "arbitrary"), vmem_limit_bytes=VMEM_LIMIT),
        name="gdn",
    )(xc, z, ab, a_log, dt_bias, out_gain)


_PAIR = LANES // MOBA_HEAD_DIM
_MOBA_STEP_HEADS = 8
_MOBA_TRIP_BLOCKS = 1
_VT_ROWS = MOBA_HEAD_DIM + 16
_LOG2E = math.log2(math.e)


def _norm_rope_pairs(xs, gain, cos_t, sin_t):
    d = MOBA_HEAD_DIM
    lane = lax.broadcasted_iota(jnp.int32, (1, LANES), 1)
    ri = lax.broadcasted_iota(jnp.int32, (LANES, LANES), 0)
    ci = lax.broadcasted_iota(jnp.int32, (LANES, LANES), 1)
    same_head = jnp.where((ri // d) == (ci // d), 1.0, 0.0).astype(BF16)
    parts = [_split(x * x) for x in xs]
    sums = [_bdot(hi, same_head) + _bdot(lo, same_head) for hi, lo in parts]
    half = ROPE_DIM // 2
    out = []
    for x, ssq in zip(xs, sums):
        xn = x * lax.rsqrt(ssq * (1.0 / d) + NORM_EPS) * gain
        partner = jnp.where((lane & (d - 1)) < half,
                            pltpu.roll(xn, LANES - half, axis=1),
                            pltpu.roll(xn, half, axis=1))
        out.append(xn * cos_t + partner * sin_t)
    return out


def _moba_kernel(q_ref, k_ref, v_ref, qg_ref, kg_ref, cq_ref, sq_ref, ck_ref, sk_ref, o_ref,
                 k_sc, vt_sc, km_sc, p_sc, *, n_blocks, n_heads):
    blk, d, nt = MOBA_BLOCK, MOBA_HEAD_DIM, _MOBA_TRIP_BLOCKS
    qb = pl.program_id(2)
    neg_inf = -jnp.inf
    heads = range(n_heads)
    pairs = range(n_heads // _PAIR)

    def slab(ref_rows, pr):
        return ref_rows[:, pr * LANES:(pr + 1) * LANES]

    @pl.when(qb == 0)
    def _prep_keys():
        extra = lax.broadcasted_iota(jnp.int32, (_VT_ROWS - d, blk), 0)
        ones_row = jnp.where(extra == 0, 1.0, 0.0).astype(BF16)
        for j in range(n_blocks):
            rows = slice(j * blk, (j + 1) * blk)
            k_rows = k_ref[0, rows, :]
            v_rows = v_ref[0, rows, :]
            kns = _norm_rope_pairs([slab(k_rows, pr).astype(F32) for pr in pairs], kg_ref[...], ck_ref[rows, :],
                                   sk_ref[rows, :])
            for pr in pairs:
                km_sc[pr, j:j + 1, :] = jnp.mean(kns[pr], axis=0, keepdims=True)
                k_sc[pr, j] = kns[pr].astype(BF16)
                vt = slab(v_rows, pr).astype(F32).T
                for h2 in range(_PAIR):
                    hh = pr * _PAIR + h2
                    vt_sc[hh, j, 0:d, :] = vt[h2 * d:(h2 + 1) * d, :].astype(BF16)
                    vt_sc[hh, j, d:, :] = ones_row

    blk_id = lax.broadcasted_iota(jnp.int32, (n_blocks, blk), 0)
    blk_f = blk_id.astype(F32)
    key_i = lax.broadcasted_iota(jnp.int32, (blk, blk), 0)
    qry_i = lax.broadcasted_iota(jnp.int32, (blk, blk), 1)
    dim_i = lax.broadcasted_iota(jnp.int32, (LANES, blk), 0)

    q_rows = q_ref[0]
    qt_pairs = [y.T for y in _norm_rope_pairs([slab(q_rows, pr) for pr in pairs], qg_ref[...], cq_ref[...],
                                              sq_ref[...])]
    qts = [jnp.where((dim_i // d) == hh % _PAIR, qt_pairs[hh // _PAIR], 0.0) for hh in heads]
    gates = [jnp.where(blk_id < qb, _dot3(km_sc[hh // _PAIR], qts[hh]), neg_inf) for hh in heads]
    picks = []
    for gate in gates:
        pk = []
        for r in range(MOBA_TOPK):
            top = jnp.max(gate, axis=0, keepdims=True)
            idx = jnp.min(jnp.where(gate == top, blk_f, float(n_blocks)), axis=0, keepdims=True)
            pk.append(jnp.where(r < qb, idx, -1.0))
            gate = jnp.where(blk_f == idx, neg_inf, gate)
        picks.append(pk)
    q_scaled = [(qt * (d ** -0.5 * _LOG2E)).astype(BF16) for qt in qts]
    s_own = [_bdot(k_sc[hh // _PAIR, qb], q_scaled[hh]) for hh in heads]
    stats0 = []
    for hh in heads:
        s_t = jnp.where(key_i <= qry_i, s_own[hh], neg_inf)
        m = jnp.max(s_t, axis=0, keepdims=True)
        p_sc[hh, 0] = jnp.exp2(s_t - m).astype(BF16)
        for a in range(1, nt):
            p_sc[hh, a] = jnp.zeros((blk, blk), BF16)
        stats0.append((m, jnp.zeros((_VT_ROWS, blk), F32)))

    def pending_block(t, a):
        return jnp.where(t == 0, qb, jnp.minimum((t - 1) * nt + a, n_blocks - 1))

    def pending_pv(t, hh):
        pv = None
        for a in range(nt):
            term = _bdot(vt_sc[hh, pending_block(t, a)], p_sc[hh, a])
            pv = term if pv is None else pv + term
        return pv

    def trip(t, carry):
        s_all = [[_bdot(k_sc[hh // _PAIR, jnp.minimum(t * nt + a, n_blocks - 1)], q_scaled[hh]) for a in range(nt)]
                 for hh in heads]
        pv_all = [pending_pv(t, hh) for hh in heads]
        out = []
        for hh in heads:
            m, acc = carry[hh]
            pk = picks[hh]
            chosen, m_new = [], m
            for a in range(nt):
                jf = (t * nt + a).astype(F32)
                chosen.append((pk[0] == jf) | (pk[1] == jf) | (pk[2] == jf))
                m_new = jnp.where(chosen[a], jnp.maximum(m_new, jnp.max(s_all[hh][a], axis=0, keepdims=True)), m_new)
            for a in range(nt):
                shift = jnp.where(chosen[a], m_new, jnp.inf)
                p_sc[hh, a] = jnp.exp2(s_all[hh][a] - shift).astype(BF16)
            out.append((m_new, jnp.exp2(m - m_new) * (acc + pv_all[hh])))
        return tuple(out)

    n_trips = (qb + nt - 1) // nt
    final = lax.fori_loop(0, n_trips, trip, tuple(stats0))
    outs = []
    for hh, (_, acc) in enumerate(final):
        acc = acc + pending_pv(n_trips, hh)
        outs.append(acc[:d] / acc[d:d + 1])
    o_ref[0] = jnp.concatenate(outs, axis=0).T


def _moba(mq, mk, mv, q_gain, k_gain, cos_t, sin_t, *, batch, seq):
    n_blocks = seq // MOBA_BLOCK
    n_heads = _MOBA_STEP_HEADS
    cols = n_heads * MOBA_HEAD_DIM
    shape3 = (batch, seq, MOBA_WIDTH)
    q_spec = pl.BlockSpec((1, MOBA_BLOCK, cols), lambda b, g, i: (b, i, g))
    kv_spec = pl.BlockSpec((1, seq, cols), lambda b, g, i: (b, 0, g))
    out = pl.pallas_call(
        functools.partial(_moba_kernel, n_blocks=n_blocks, n_heads=n_heads),
        grid=(batch, MOBA_HEADS // n_heads, n_blocks),
        in_specs=[
            q_spec, kv_spec, kv_spec,
            _const_spec((1, LANES)), _const_spec((1, LANES)),
            pl.BlockSpec((MOBA_BLOCK, LANES), lambda b, g, i: (i, 0)),
            pl.BlockSpec((MOBA_BLOCK, LANES), lambda b, g, i: (i, 0)),
            _const_spec((seq, LANES)), _const_spec((seq, LANES)),
        ],
        out_specs=q_spec,
        out_shape=jax.ShapeDtypeStruct(shape3, F32),
        scratch_shapes=[
            pltpu.VMEM((n_heads // _PAIR, n_blocks, MOBA_BLOCK, LANES), BF16),
            pltpu.VMEM((n_heads, n_blocks, _VT_ROWS, MOBA_BLOCK), BF16),
            pltpu.VMEM((n_heads // _PAIR, n_blocks, LANES), F32),
            pltpu.VMEM((n_heads, _MOBA_TRIP_BLOCKS, MOBA_BLOCK, MOBA_BLOCK), BF16),
        ],
        compiler_params=pltpu.CompilerParams(
            dimension_semantics=("arbitrary", "arbitrary", "arbitrary"), vmem_limit_bytes=VMEM_LIMIT),
        name="moba",
    )(mq.reshape(shape3), mk.reshape(shape3), mv.reshape(shape3), q_gain, k_gain, cos_t, sin_t, cos_t, sin_t)
    return out.reshape(batch * seq, MOBA_WIDTH)


def _rope_tables(seq):
    half = ROPE_DIM // 2
    inv_freq = jnp.power(jnp.float32(ROPE_THETA), -jnp.arange(half, dtype=F32) * 2.0 / ROPE_DIM)
    ang = jnp.arange(seq).astype(F32)[:, None] * inv_freq[None, :]
    cos, sin = jnp.cos(ang), jnp.sin(ang)
    rest = MOBA_HEAD_DIM - ROPE_DIM
    cos_h = jnp.concatenate([cos, cos, jnp.ones((seq, rest), F32)], axis=-1)
    sin_h = jnp.concatenate([-sin, sin, jnp.zeros((seq, rest), F32)], axis=-1)
    return jnp.tile(cos_h, (1, _PAIR)), jnp.tile(sin_h, (1, _PAIR))


def kernel(x, ffn1_norm, ffn1_w_gate, ffn1_w_up, ffn1_w_down, mix_norm, w_in, gdn_conv, gdn_a_log, gdn_dt_bias,
           gdn_out_norm, moba_q_norm, moba_k_norm, w_out, ffn2_norm, ffn2_w_gate, ffn2_w_up, ffn2_w_down):
    batch, seq, d_model = x.shape
    depth = w_in.shape[0]
    off_z = 3 * GDN_WIDTH
    off_ab = 4 * GDN_WIDTH
    off_moba = off_ab + 2 * GDN_HEADS
    cos_t, sin_t = _rope_tables(seq)
    xf = x.reshape(batch * seq, d_model)
    for l in range(depth):
        xf = _ffn(xf, ffn1_norm[l][None], ffn1_w_gate[l].astype(BF16), ffn1_w_up[l].astype(BF16),
                  ffn1_w_down[l].astype(BF16))
        w = w_in[l]
        w_ab = jnp.pad(w[:, off_ab:off_moba], ((0, 0), (0, LANES - 2 * GDN_HEADS)))
        w_all = jnp.concatenate([w[:, :off_ab], w_ab, w[:, off_moba:]], axis=1).astype(BF16)
        xc, z, ab, mq, mk, mv = _inproj(xf, mix_norm[l][None], w_all, gdn_conv[l].astype(F32), seq=seq)
        a_log = jnp.pad(gdn_a_log[l].astype(F32), (0, LANES - GDN_HEADS))[None]
        dt_b = jnp.pad(gdn_dt_bias[l].astype(F32), (0, LANES - GDN_HEADS))[None]
        o_gdn = _gdn(xc, z, ab, a_log, dt_b, gdn_out_norm[l][None], batch=batch, seq=seq)
        o_moba = _moba(mq, mk, mv, jnp.tile(moba_q_norm[l], _PAIR)[None], jnp.tile(moba_k_norm[l], _PAIR)[None],
                       cos_t, sin_t, batch=batch, seq=seq)
        wo = w_out[l].astype(BF16)
        xf = _ffn(xf, ffn2_norm[l][None], ffn2_w_gate[l].astype(BF16), ffn2_w_up[l].astype(BF16),
                  ffn2_w_down[l].astype(BF16), mix=(o_gdn, o_moba), mix_w=(wo[:GDN_WIDTH], wo[GDN_WIDTH:]))
    return xf.reshape(batch, seq, d_model)
```

```python
import functools
import math

import jax
import jax.numpy as jnp
from jax import lax
from jax.experimental import pallas as pl
from jax.experimental.pallas import tpu as pltpu

F32 = jnp.float32
BF16 = jnp.bfloat16

NORM_EPS = 1e-6
GDN_HEADS = 4
GDN_HEAD_DIM = 128
GDN_WIDTH = GDN_HEADS * GDN_HEAD_DIM
GDN_CHUNK = 64
CONV_WIDTH = 4
MOBA_HEADS = 8
MOBA_HEAD_DIM = 64
MOBA_WIDTH = MOBA_HEADS * MOBA_HEAD_DIM
MOBA_BLOCK = 256
MOBA_TOPK = 3
ROPE_THETA = 500000.0
ROPE_DIM = MOBA_HEAD_DIM // 4
LANES = 128
MXU_TILE = 256
CONV_TAIL = 8
VMEM_LIMIT = 56 * 1024 * 1024

def _bdot(a, b, dims=None):
    a = a.astype(BF16)
    b = b.astype(BF16)
    if dims is None:
        return jnp.dot(a, b, preferred_element_type=F32)
    return lax.dot_general(a, b, dims, preferred_element_type=F32)


def _split(a):
    hi = a.astype(BF16)
    lo = (a - hi.astype(F32)).astype(BF16)
    return hi, lo


def _dot3(a, b, dims=None):
    a_hi, a_lo = _split(a)
    b_hi, b_lo = _split(b)
    return _bdot(a_hi, b_hi, dims) + (_bdot(a_hi, b_lo, dims) + _bdot(a_lo, b_hi, dims))


def _silu(x):
    return x * jax.nn.sigmoid(x)


def _softplus(x):
    return jnp.maximum(x, 0.0) + jnp.log1p(jnp.exp(-jnp.abs(x)))


def _rms_norm(x, g):
    return x * lax.rsqrt(jnp.mean(x * x, axis=-1, keepdims=True) + NORM_EPS) * g


def _ffn_kernel(*refs, n_split, n_mix):
    x_ref = refs[0]
    mix_refs = refs[1:1 + 2 * n_mix]
    g_ref, wg_ref, wu_ref, wd_ref, o_ref = refs[1 + 2 * n_mix:]
    x = x_ref[...]
    for i in range(n_mix):
        x = x + _bdot(mix_refs[i][...], mix_refs[n_mix + i][...])
    h = _rms_norm(x, g_ref[...]).astype(BF16)
    n_tiles = wg_ref.shape[1] // MXU_TILE
    cuts = [MXU_TILE * ((n_tiles * j + n_split - 1) // n_split) for j in range(n_split)] + [wg_ref.shape[1]]
    acc = None
    for lo, hi in zip(cuts[:-1], cuts[1:]):
        gate = jnp.dot(h, wg_ref[:, lo:hi], preferred_element_type=F32)
        up = jnp.dot(h, wu_ref[:, lo:hi], preferred_element_type=F32)
        act = (_silu(gate) * up).astype(BF16)
        y = jnp.dot(act, wd_ref[lo:hi, :], preferred_element_type=F32)
        acc = y if acc is None else acc + y
    o_ref[...] = x + 0.5 * acc


def _const_spec(shape):
    return pl.BlockSpec(shape, lambda *_: (0,) * len(shape), pipeline_mode=pl.Buffered(1))


def _ffn(x, norm_g, wg, wu, wd, mix=(), mix_w=(), *, tm=512, n_split=2):
    t, d = x.shape
    ff = wg.shape[1]
    row_spec = lambda width: pl.BlockSpec((tm, width), lambda i: (i, 0))
    return pl.pallas_call(
        functools.partial(_ffn_kernel, n_split=n_split, n_mix=len(mix)),
        grid=(t // tm,),
        in_specs=[row_spec(d)] + [row_spec(m.shape[1]) for m in mix] + [_const_spec(w.shape) for w in mix_w] + [
            _const_spec((1, d)),
            _const_spec((d, ff)),
            _const_spec((d, ff)),
            _const_spec((ff, d)),
        ],
        out_specs=row_spec(d),
        out_shape=jax.ShapeDtypeStruct((t, d), F32),
        compiler_params=pltpu.CompilerParams(dimension_semantics=("parallel",), vmem_limit_bytes=VMEM_LIMIT),
        name="ffn",
    )(x, *mix, *mix_w, norm_g, wg, wu, wd)


_IN_SPLITS = (3 * GDN_WIDTH, GDN_WIDTH, LANES, MOBA_WIDTH, MOBA_WIDTH, MOBA_WIDTH)
_IN_DTYPES = (F32, F32, F32, F32, F32, BF16)


def _inproj_kernel(x_ref, g_ref, w_ref, cw_ref, *refs, tiles_per_seq):
    *out_refs, raw_sc = refs
    i = pl.program_id(0)
    tm = x_ref.shape[0]
    conv_cols = _IN_SPLITS[0]

    @pl.when(i == 0)
    def _nothing_pending():
        raw_sc[...] = jnp.zeros_like(raw_sc)

    win = raw_sc[...]
    cw = cw_ref[...]
    conv = win * cw[CONV_WIDTH - 1:CONV_WIDTH]
    for back in range(1, CONV_WIDTH):
        conv = conv + pltpu.roll(win, back, axis=0) * cw[CONV_WIDTH - 1 - back:CONV_WIDTH - back]
    out_refs[0][...] = _silu(conv[CONV_TAIL:])

    h = _rms_norm(x_ref[...], g_ref[...]).astype(BF16)
    p = jnp.dot(h, w_ref[...], preferred_element_type=F32)
    new_sequence = (i % tiles_per_seq) == 0
    raw_sc[0:CONV_TAIL, :] = jnp.where(new_sequence, 0.0, win[tm:])
    raw_sc[CONV_TAIL:, :] = p[:, :conv_cols]
    off = conv_cols
    for ref, width in zip(out_refs[1:], _IN_SPLITS[1:]):
        ref[...] = p[:, off:off + width].astype(ref.dtype)
        off += width


def _inproj(x, norm_g, w_all, conv_w, *, seq, tm=512):
    t, d = x.shape
    n = w_all.shape[1]
    last = t // tm - 1
    cur_map = lambda i: (jnp.minimum(i, last), 0)
    lag_map = lambda i: (jnp.maximum(i - 1, 0), 0)
    return pl.pallas_call(
        functools.partial(_inproj_kernel, tiles_per_seq=seq // tm),
        grid=(t // tm + 1,),
        in_specs=[pl.BlockSpec((tm, d), cur_map), _const_spec((1, d)), _const_spec((d, n)),
                  _const_spec(conv_w.shape)],
        out_specs=[pl.BlockSpec((tm, _IN_SPLITS[0]), lag_map)] + [pl.BlockSpec((tm, w), cur_map) for w in _IN_SPLITS[1:]],
        out_shape=[jax.ShapeDtypeStruct((t, w), dt) for w, dt in zip(_IN_SPLITS, _IN_DTYPES)],
        scratch_shapes=[pltpu.VMEM((CONV_TAIL + tm, _IN_SPLITS[0]), F32)],
        compiler_params=pltpu.CompilerParams(dimension_semantics=("arbitrary",), vmem_limit_bytes=VMEM_LIMIT),
        name="inproj",
    )(x, norm_g, w_all, conv_w)


_GDN_GROUP = 4
_BNT = (((2,), (2,)), ((0,), (0,)))
_BNN = (((2,), (1,)), ((0,), (0,)))
_BTN = (((1,), (1,)), ((0,), (0,)))
_HANDOVER = object()


def _chunk_cumsum(x, c_len):
    pos = lax.broadcasted_iota(jnp.int32, x.shape, 0) & (c_len - 1)
    s = 1
    while s < c_len:
        x = x + jnp.where(pos >= s, pltpu.roll(x, s, axis=0), 0.0)
        s *= 2
    return x


def _unit_lower_inverse_rounds(lower, out, base=16):
    c = lower.shape[-1]
    ri = lax.broadcasted_iota(jnp.int32, (1, c, c), 1)
    ci = lax.broadcasted_iota(jnp.int32, (1, c, c), 2)
    same = lambda size: (ri // size) == (ci // size)
    neg = jnp.where(same(base), -lower, 0.0)
    inv = jnp.where(ri == ci, 1.0, 0.0) + neg
    pw = _bdot(neg, neg, _BNN)
    yield
    span = 2
    while 2 * span < base:
        step = _bdot(pw, inv, _BNN)
        pw = _bdot(pw, pw, _BNN)
        yield
        inv = inv + step
        span *= 2
    step = _bdot(pw, inv, _BNN)
    yield
    inv = inv + step
    size = base
    while size < c:
        off = jnp.where(same(size), 0.0, jnp.where(same(2 * size), lower, 0.0))
        left = _bdot(inv, off, _BNN)
        yield
        corr = _bdot(left, inv, _BNN)
        yield
        inv = inv - corr
        size *= 2
    out.append(inv)


def _gdn_kernel(xc_ref, z_ref, ab_ref, alog_ref, dtb_ref, og_ref, o_ref, state_sc,
                u_sc, w_sc, qd_sc, kd_sc, intra_sc, el_sc, *, n_groups):
    c_len, dh, n_heads, width, grp = GDN_CHUNK, GDN_HEAD_DIM, GDN_HEADS, GDN_WIDTH, _GDN_GROUP
    rows = grp * c_len

    @pl.when(pl.program_id(1) == 0)
    def _reset():
        state_sc[...] = jnp.zeros_like(state_sc)

    ri = lax.broadcasted_iota(jnp.int32, (1, c_len, c_len), 1)
    ci = lax.broadcasted_iota(jnp.int32, (1, c_len, c_len), 2)
    causal = ri >= ci
    strict = ri > ci
    og = og_ref[...]

    def tiles(x, off):
        return jnp.stack([x[g * c_len:(g + 1) * c_len, off + h * dh:off + (h + 1) * dh]
                          for g in range(grp) for h in range(n_heads)])

    def lane_tiles(x, lane0):
        return jnp.stack([jnp.broadcast_to(x[g * c_len:(g + 1) * c_len, lane0 + h:lane0 + h + 1], (c_len, dh))
                          for g in range(grp) for h in range(n_heads)])

    def local_rounds(gi):
        r0 = gi * rows if isinstance(gi, int) else pl.multiple_of(gi * rows, rows)
        xc = xc_ref[pl.ds(r0, rows), :]
        ab = ab_ref[pl.ds(r0, rows), :]
        g_tok = -jnp.exp(alog_ref[...]) * _softplus(ab + dtb_ref[...])
        gc_tok = _chunk_cumsum(g_tok, c_len)
        gc = lane_tiles(gc_tok, 0)
        gc_t = gc_tok.T
        gc_row = jnp.stack([jnp.broadcast_to(gc_t[h:h + 1, g * c_len:(g + 1) * c_len], (c_len, c_len))
                            for g in range(grp) for h in range(n_heads)])
        beta = lane_tiles(jax.nn.sigmoid(ab), n_heads)
        q = tiles(xc, 0)
        k = tiles(xc, width)
        v = tiles(xc, 2 * width)
        q = q * lax.rsqrt(jnp.sum(q * q, axis=-1, keepdims=True) + NORM_EPS) * (dh ** -0.5)
        k = k * lax.rsqrt(jnp.sum(k * k, axis=-1, keepdims=True) + NORM_EPS)
        g_last = gc[:, c_len - 1:c_len, :]
        e_gc = jnp.exp(gc)
        decay = jnp.exp(jnp.where(causal, gc[:, :, :c_len] - gc_row, -jnp.inf))
        k_beta = k * beta
        kk = _bdot(k_beta, k, _BNT)
        qk = _bdot(q, k, _BNT)
        yield
        lower = jnp.where(strict, kk * decay, 0.0)
        inv_out = []
        yield from _unit_lower_inverse_rounds(lower, inv_out)
        sol = _bdot(inv_out[0], jnp.concatenate([v * beta, k_beta * e_gc], axis=-1), _BNN)
        yield
        yield _HANDOVER
        u_sc[...] = sol[:, :, :dh]
        w_sc[...] = sol[:, :, dh:].astype(BF16)
        qd_sc[...] = (q * e_gc).astype(BF16)
        kd_sc[...] = (k * jnp.exp(g_last - gc)).astype(BF16)
        intra_sc[...] = (qk * decay).astype(BF16)
        el_sc[...] = jnp.exp(g_last)

    def recurrence_rounds(gi):
        r0 = gi * rows if isinstance(gi, int) else pl.multiple_of(gi * rows, rows)
        state = state_sc[...]
        for g in range(grp):
            sl = slice(g * n_heads, (g + 1) * n_heads)
            w_s = _bdot(w_sc[sl], state, _BNN)
            q_s = _bdot(qd_sc[sl], state, _BNN)
            yield
            v_new = u_sc[sl] - w_s
            a_v = _bdot(intra_sc[sl], v_new, _BNN)
            k_v = _bdot(kd_sc[sl], v_new, _BTN)
            yield
            state = state * el_sc[sl] + k_v
            on = _rms_norm(q_s + a_v, og)
            zz = _silu(z_ref[pl.ds(r0 + g * c_len, c_len), :])
            o_ref[pl.ds(r0 + g * c_len, c_len), :] = jnp.concatenate([on[h] for h in range(n_heads)], axis=-1) * zz
        state_sc[...] = state

    def interleave(local, recurrence=()):
        recurrence = iter(recurrence)
        for tag in local:
            if tag is _HANDOVER:
                for _ in recurrence:
                    pass
            else:
                next(recurrence, None)
        for _ in recurrence:
            pass

    interleave(local_rounds(0))

    def pipelined(gi, carry):
        interleave(local_rounds(gi), recurrence_rounds(gi - 1))
        return carry

    lax.fori_loop(1, n_groups, pipelined, 0)
    interleave((), recurrence_rounds(n_groups - 1))


def _gdn(xc, z, ab, a_log, dt_bias, out_gain, *, batch, seq, seg=2048):
    t = batch * seq
    n_seg = seq // seg
    n_tiles = _GDN_GROUP * GDN_HEADS
    row_map = lambda b, s: (b * n_seg + s, 0)
    return pl.pallas_call(
        functools.partial(_gdn_kernel, n_groups=seg // (GDN_CHUNK * _GDN_GROUP)),
        grid=(batch, n_seg),
        in_specs=[
            pl.BlockSpec((seg, 3 * GDN_WIDTH), row_map),
            pl.BlockSpec((seg, GDN_WIDTH), row_map),
            pl.BlockSpec((seg, LANES), row_map),
            _const_spec((1, LANES)),
            _const_spec((1, LANES)),
            _const_spec((1, GDN_HEAD_DIM)),
        ],
        out_specs=pl.BlockSpec((seg, GDN_WIDTH), row_map),
        out_shape=jax.ShapeDtypeStruct((t, GDN_WIDTH), F32),
        scratch_shapes=[
            pltpu.VMEM((GDN_HEADS, GDN_HEAD_DIM, GDN_HEAD_DIM), F32),
            pltpu.VMEM((n_tiles, GDN_CHUNK, GDN_HEAD_DIM), F32),
            pltpu.VMEM((n_tiles, GDN_CHUNK, GDN_HEAD_DIM), BF16),
            pltpu.VMEM((n_tiles, GDN_CHUNK, GDN_HEAD_DIM), BF16),
            pltpu.VMEM((n_tiles, GDN_CHUNK, GDN_HEAD_DIM), BF16),
            pltpu.VMEM((n_tiles, GDN_CHUNK, GDN_CHUNK), BF16),
            pltpu.VMEM((n_tiles, 1, GDN_HEAD_DIM), F32),
        ],
        compiler_params=pltpu.CompilerParams(
            dimension_semantics=("arbitrary", "arbitrary"), vmem_limit_bytes=VMEM_LIMIT),
        name="gdn",
    )(xc, z, ab, a_log, dt_bias, out_gain)


_PAIR = LANES // MOBA_HEAD_DIM
_MOBA_STEP_HEADS = 8
_MOBA_TRIP_BLOCKS = 1
_VT_ROWS = MOBA_HEAD_DIM + 16
_LOG2E = math.log2(math.e)


def _norm_rope_pairs(xs, gain, cos_t, sin_t):
    d = MOBA_HEAD_DIM
    lane = lax.broadcasted_iota(jnp.int32, (1, LANES), 1)
    ri = lax.broadcasted_iota(jnp.int32, (LANES, LANES), 0)
    ci = lax.broadcasted_iota(jnp.int32, (LANES, LANES), 1)
    same_head = jnp.where((ri // d) == (ci // d), 1.0, 0.0).astype(BF16)
    parts = [_split(x * x) for x in xs]
    sums = [_bdot(hi, same_head) + _bdot(lo, same_head) for hi, lo in parts]
    half = ROPE_DIM // 2
    out = []
    for x, ssq in zip(xs, sums):
        xn = x * lax.rsqrt(ssq * (1.0 / d) + NORM_EPS) * gain
        partner = jnp.where((lane & (d - 1)) < half,
                            pltpu.roll(xn, LANES - half, axis=1),
                            pltpu.roll(xn, half, axis=1))
        out.append(xn * cos_t + partner * sin_t)
    return out


def _moba_kernel(q_ref, k_ref, v_ref, qg_ref, kg_ref, cq_ref, sq_ref, ck_ref, sk_ref, o_ref,
                 k_sc, vt_sc, km_sc, p_sc, *, n_blocks, n_heads):
    blk, d, nt = MOBA_BLOCK, MOBA_HEAD_DIM, _MOBA_TRIP_BLOCKS
    qb = pl.program_id(2)
    neg_inf = -jnp.inf
    heads = range(n_heads)
    pairs = range(n_heads // _PAIR)

    def slab(ref_rows, pr):
        return ref_rows[:, pr * LANES:(pr + 1) * LANES]

    @pl.when(qb == 0)
    def _prep_keys():
        extra = lax.broadcasted_iota(jnp.int32, (_VT_ROWS - d, blk), 0)
        ones_row = jnp.where(extra == 0, 1.0, 0.0).astype(BF16)
        for j in range(n_blocks):
            rows = slice(j * blk, (j + 1) * blk)
            k_rows = k_ref[0, rows, :]
            v_rows = v_ref[0, rows, :]
            kns = _norm_rope_pairs([slab(k_rows, pr).astype(F32) for pr in pairs], kg_ref[...], ck_ref[rows, :],
                                   sk_ref[rows, :])
            for pr in pairs:
                km_sc[pr, j:j + 1, :] = jnp.mean(kns[pr], axis=0, keepdims=True)
                k_sc[pr, j] = kns[pr].astype(BF16)
                vt = slab(v_rows, pr).astype(F32).T
                for h2 in range(_PAIR):
                    hh = pr * _PAIR + h2
                    vt_sc[hh, j, 0:d, :] = vt[h2 * d:(h2 + 1) * d, :].astype(BF16)
                    vt_sc[hh, j, d:, :] = ones_row

    blk_id = lax.broadcasted_iota(jnp.int32, (n_blocks, blk), 0)
    blk_f = blk_id.astype(F32)
    key_i = lax.broadcasted_iota(jnp.int32, (blk, blk), 0)
    qry_i = lax.broadcasted_iota(jnp.int32, (blk, blk), 1)
    dim_i = lax.broadcasted_iota(jnp.int32, (LANES, blk), 0)

    q_rows = q_ref[0]
    qt_pairs = [y.T for y in _norm_rope_pairs([slab(q_rows, pr) for pr in pairs], qg_ref[...], cq_ref[...],
                                              sq_ref[...])]
    qts = [jnp.where((dim_i // d) == hh % _PAIR, qt_pairs[hh // _PAIR], 0.0) for hh in heads]
    gates = [jnp.where(blk_id < qb, _dot3(km_sc[hh // _PAIR], qts[hh]), neg_inf) for hh in heads]
    picks = []
    for gate in gates:
        pk = []
        for r in range(MOBA_TOPK):
            top = jnp.max(gate, axis=0, keepdims=True)
            idx = jnp.min(jnp.where(gate == top, blk_f, float(n_blocks)), axis=0, keepdims=True)
            pk.append(jnp.where(r < qb, idx, -1.0))
            gate = jnp.where(blk_f == idx, neg_inf, gate)
        picks.append(pk)
    q_scaled = [(qt * (d ** -0.5 * _LOG2E)).astype(BF16) for qt in qts]
    s_own = [_bdot(k_sc[hh // _PAIR, qb], q_scaled[hh]) for hh in heads]
    stats0 = []
    for hh in heads:
        s_t = jnp.where(key_i <= qry_i, s_own[hh], neg_inf)
        m = jnp.max(s_t, axis=0, keepdims=True)
        p_sc[hh, 0] = jnp.exp2(s_t - m).astype(BF16)
        for a in range(1, nt):
            p_sc[hh, a] = jnp.zeros((blk, blk), BF16)
        stats0.append((m, jnp.zeros((_VT_ROWS, blk), F32)))

    def pending_block(t, a):
        return jnp.where(t == 0, qb, jnp.minimum((t - 1) * nt + a, n_blocks - 1))

    def pending_pv(t, hh):
        pv = None
        for a in range(nt):
            term = _bdot(vt_sc[hh, pending_block(t, a)], p_sc[hh, a])
            pv = term if pv is None else pv + term
        return pv

    def trip(t, carry):
        s_all = [[_bdot(k_sc[hh // _PAIR, jnp.minimum(t * nt + a, n_blocks - 1)], q_scaled[hh]) for a in range(nt)]
                 for hh in heads]
        pv_all = [pending_pv(t, hh) for hh in heads]
        out = []
        for hh in heads:
            m, acc = carry[hh]
            pk = picks[hh]
            chosen, m_new = [], m
            for a in range(nt):
                jf = (t * nt + a).astype(F32)
                chosen.append((pk[0] == jf) | (pk[1] == jf) | (pk[2] == jf))
                m_new = jnp.where(chosen[a], jnp.maximum(m_new, jnp.max(s_all[hh][a], axis=0, keepdims=True)), m_new)
            for a in range(nt):
                shift = jnp.where(chosen[a], m_new, jnp.inf)
                p_sc[hh, a] = jnp.exp2(s_all[hh][a] - shift).astype(BF16)
            out.append((m_new, jnp.exp2(m - m_new) * (acc + pv_all[hh])))
        return tuple(out)

    n_trips = (qb + nt - 1) // nt
    final = lax.fori_loop(0, n_trips, trip, tuple(stats0))
    outs = []
    for hh, (_, acc) in enumerate(final):
        acc = acc + pending_pv(n_trips, hh)
        outs.append(acc[:d] / acc[d:d + 1])
    o_ref[0] = jnp.concatenate(outs, axis=0).T


def _moba(mq, mk, mv, q_gain, k_gain, cos_t, sin_t, *, batch, seq):
    n_blocks = seq // MOBA_BLOCK
    n_heads = _MOBA_STEP_HEADS
    cols = n_heads * MOBA_HEAD_DIM
    shape3 = (batch, seq, MOBA_WIDTH)
    q_spec = pl.BlockSpec((1, MOBA_BLOCK, cols), lambda b, g, i: (b, i, g))
    kv_spec = pl.BlockSpec((1, seq, cols), lambda b, g, i: (b, 0, g))
    out = pl.pallas_call(
        functools.partial(_moba_kernel, n_blocks=n_blocks, n_heads=n_heads),
        grid=(batch, MOBA_HEADS // n_heads, n_blocks),
        in_specs=[
            q_spec, kv_spec, kv_spec,
            _const_spec((1, LANES)), _const_spec((1, LANES)),
            pl.BlockSpec((MOBA_BLOCK, LANES), lambda b, g, i: (i, 0)),
            pl.BlockSpec((MOBA_BLOCK, LANES), lambda b, g, i: (i, 0)),
            _const_spec((seq, LANES)), _const_spec((seq, LANES)),
        ],
        out_specs=q_spec,
        out_shape=jax.ShapeDtypeStruct(shape3, F32),
        scratch_shapes=[
            pltpu.VMEM((n_heads // _PAIR, n_blocks, MOBA_BLOCK, LANES), BF16),
            pltpu.VMEM((n_heads, n_blocks, _VT_ROWS, MOBA_BLOCK), BF16),
            pltpu.VMEM((n_heads // _PAIR, n_blocks, LANES), F32),
            pltpu.VMEM((n_heads, _MOBA_TRIP_BLOCKS, MOBA_BLOCK, MOBA_BLOCK), BF16),
        ],
        compiler_params=pltpu.CompilerParams(
            dimension_semantics=("arbitrary", "arbitrary", "arbitrary"), vmem_limit_bytes=VMEM_LIMIT),
        name="moba",
    )(mq.reshape(shape3), mk.reshape(shape3), mv.reshape(shape3), q_gain, k_gain, cos_t, sin_t, cos_t, sin_t)
    return out.reshape(batch * seq, MOBA_WIDTH)


def _rope_tables(seq):
    half = ROPE_DIM // 2
    inv_freq = jnp.power(jnp.float32(ROPE_THETA), -jnp.arange(half, dtype=F32) * 2.0 / ROPE_DIM)
    ang = jnp.arange(seq).astype(F32)[:, None] * inv_freq[None, :]
    cos, sin = jnp.cos(ang), jnp.sin(ang)
    rest = MOBA_HEAD_DIM - ROPE_DIM
    cos_h = jnp.concatenate([cos, cos, jnp.ones((seq, rest), F32)], axis=-1)
    sin_h = jnp.concatenate([-sin, sin, jnp.zeros((seq, rest), F32)], axis=-1)
    return jnp.tile(cos_h, (1, _PAIR)), jnp.tile(sin_h, (1, _PAIR))


def kernel(x, ffn1_norm, ffn1_w_gate, ffn1_w_up, ffn1_w_down, mix_norm, w_in, gdn_conv, gdn_a_log, gdn_dt_bias,
           gdn_out_norm, moba_q_norm, moba_k_norm, w_out, ffn2_norm, ffn2_w_gate, ffn2_w_up, ffn2_w_down):
    batch, seq, d_model = x.shape
    depth = w_in.shape[0]
    off_ab = 4 * GDN_WIDTH
    off_moba = off_ab + 2 * GDN_HEADS
    cos_t, sin_t = _rope_tables(seq)
    xf = x.reshape(batch * seq, d_model)
    for l in range(depth):
        xf = _ffn(xf, ffn1_norm[l][None], ffn1_w_gate[l].astype(BF16), ffn1_w_up[l].astype(BF16),
                  ffn1_w_down[l].astype(BF16))
        w = w_in[l]
        w_ab = jnp.pad(w[:, off_ab:off_moba], ((0, 0), (0, LANES - 2 * GDN_HEADS)))
        w_all = jnp.concatenate([w[:, :off_ab], w_ab, w[:, off_moba:]], axis=1).astype(BF16)
        xc, z, ab, mq, mk, mv = _inproj(xf, mix_norm[l][None], w_all, gdn_conv[l].astype(F32), seq=seq)
        a_log = jnp.pad(gdn_a_log[l].astype(F32), (0, LANES - GDN_HEADS))[None]
        dt_b = jnp.pad(gdn_dt_bias[l].astype(F32), (0, LANES - GDN_HEADS))[None]
        o_gdn = _gdn(xc, z, ab, a_log, dt_b, gdn_out_norm[l][None], batch=batch, seq=seq)
        o_moba = _moba(mq, mk, mv, jnp.tile(moba_q_norm[l], _PAIR)[None], jnp.tile(moba_k_norm[l], _PAIR)[None],
                       cos_t, sin_t, batch=batch, seq=seq)
        wo = w_out[l].astype(BF16)
        xf = _ffn(xf, ffn2_norm[l][None], ffn2_w_gate[l].astype(BF16), ffn2_w_up[l].astype(BF16),
                  ffn2_w_down[l].astype(BF16), mix=(o_gdn, o_moba), mix_w=(wo[:GDN_WIDTH], wo[GDN_WIDTH:]))
    return xf.reshape(batch, seq, d_model)
```
